```python
import math
import jax
import jax.numpy as jnp
from jax import lax
import numpy as np

D_MODEL = 1024
BATCH = 2
SEQ = 16384
DEPTH = 1
DEC_BATCH = 128
DEC_SEQ = 8
PAST_LEN = 8192
PAGE_SIZE = 128

N_HEADS = 8
HEAD_DIM = 64
D_ATTN = N_HEADS * HEAD_DIM
D_SSM = D_MODEL // 2
SSM_GROUP = 16
N_GROUPS = D_SSM // SSM_GROUP
STATE_P = 64
D_FF = 2816
CONV_W = 3
Q_BLOCK = 128
RMS_EPS = 1e-6
NEG_INF = -1e30
FORGET_BIAS = 2.0
DT_MIN = 1e-3
DT_MAX = 1e-1
SCALE = HEAD_DIM ** -0.5
IN_SIZES = (D_ATTN, D_ATTN, D_ATTN, N_HEADS, D_SSM, D_MODEL, D_MODEL)
D_IN = sum(IN_SIZES)

kernel_name = "fox_s5_gated_hybrid_step"


def rmsnorm(x, g):
    xf = x.astype(jnp.float32)
    y = xf * lax.rsqrt(jnp.mean(xf * xf, axis=-1, keepdims=True) + RMS_EPS)
    return (y * g.astype(jnp.float32)).astype(x.dtype)


def split_in_proj(z):
    idx = np.cumsum(IN_SIZES)[:-1].tolist()
    return jnp.split(z, idx, axis=-1)


def fox_attend(q, k, v, c_q, c_k, q_offset):
    B, Tq = q.shape[0], q.shape[1]
    Tk = k.shape[1]
    qb_len = Q_BLOCK if Tq % Q_BLOCK == 0 else Tq
    nb = Tq // qb_len
    ckT = jnp.swapaxes(c_k, 1, 2)
    kpos = jnp.arange(Tk)

    def block(args):
        qi, cqi, start = args
        s = jnp.einsum('bqhd,bkhd->bhqk', qi, k, preferred_element_type=jnp.float32) * SCALE
        s = s + jnp.swapaxes(cqi, 1, 2)[..., None] - ckT[:, :, None, :]
        qpos = q_offset + start + jnp.arange(qb_len)
        s = jnp.where(kpos[None, :] <= qpos[:, None], s, NEG_INF)
        p = jax.nn.softmax(s, axis=-1).astype(v.dtype)
        return jnp.einsum('bhqk,bkhd->bqhd', p, v)

    qs = jnp.moveaxis(q.reshape(B, nb, qb_len, N_HEADS, HEAD_DIM), 1, 0)
    cs = jnp.moveaxis(c_q.reshape(B, nb, qb_len, N_HEADS), 1, 0)
    starts = jnp.arange(nb) * qb_len
    out = lax.map(block, (qs, cs, starts))
    return jnp.moveaxis(out, 0, 1).reshape(B, Tq, D_ATTN)


def s5_scan(u, h0_re, h0_im, p):
    Bsz, T = u.shape[0], u.shape[1]
    ug = u.astype(jnp.float32).reshape(Bsz, T, N_GROUPS, SSM_GROUP)
    lr = p["lam_re"].astype(jnp.float32)
    li = p["lam_im"].astype(jnp.float32)
    dt = jnp.exp(p["log_dt"].astype(jnp.float32))[:, None]
    mag = jnp.exp(lr * dt)
    ang = li * dt
    ab_re = mag * jnp.cos(ang)
    ab_im = mag * jnp.sin(ang)
    den = lr * lr + li * li
    n_re = ab_re - 1.0
    n_im = ab_im
    f_re = (n_re * lr + n_im * li) / den
    f_im = (n_im * lr - n_re * li) / den
    b_re = p["b_re"].astype(jnp.float32)
    b_im = p["b_im"].astype(jnp.float32)
    bb_re = f_re[..., None] * b_re - f_im[..., None] * b_im
    bb_im = f_re[..., None] * b_im + f_im[..., None] * b_re
    x_re = jnp.einsum('gpc,btgc->btgp', bb_re, ug)
    x_im = jnp.einsum('gpc,btgc->btgp', bb_im, ug)
    h0r = h0_re.astype(jnp.float32)
    h0i = h0_im.astype(jnp.float32)
    x_re = x_re.at[:, 0].add(ab_re * h0r - ab_im * h0i)
    x_im = x_im.at[:, 0].add(ab_re * h0i + ab_im * h0r)
    a_re = jnp.broadcast_to(ab_re, x_re.shape)
    a_im = jnp.broadcast_to(ab_im, x_im.shape)

    def combine(e1, e2):
        a1r, a1i, b1r, b1i = e1
        a2r, a2i, b2r, b2i = e2
        return (a2r * a1r - a2i * a1i,
                a2r * a1i + a2i * a1r,
                a2r * b1r - a2i * b1i + b2r,
                a2r * b1i + a2i * b1r + b2i)

    _, _, h_re, h_im = lax.associative_scan(combine, (a_re, a_im, x_re, x_im), axis=1)
    y = (jnp.einsum('gcp,btgp->btgc', p["c_re"].astype(jnp.float32), h_re)
         - jnp.einsum('gcp,btgp->btgc', p["c_im"].astype(jnp.float32), h_im)
         + p["d_skip"].astype(jnp.float32).reshape(N_GROUPS, SSM_GROUP) * ug)
    return y.reshape(Bsz, T, D_SSM).astype(u.dtype), h_re[:, -1], h_im[:, -1]


def causal_dwconv(g, prev, w, b):
    T = g.shape[1]
    full = jnp.concatenate([prev.astype(g.dtype), g], axis=1)
    out = b + sum(full[:, i:i + T] * w[i] for i in range(CONV_W))
    return out, full[:, -(CONV_W - 1):]


def trunk_layer(x, past, h0_re, h0_im, conv_prev, p):
    B, T, _ = x.shape
    h = rmsnorm(x, p["g_mix"])
    q, k, v, f_logit, u, gate_a, gate_s = split_in_proj(h @ p["w_in"])
    q = q.reshape(B, T, N_HEADS, HEAD_DIM)
    k = k.reshape(B, T, N_HEADS, HEAD_DIM)
    v = v.reshape(B, T, N_HEADS, HEAD_DIM)
    logf = jax.nn.log_sigmoid(f_logit.astype(jnp.float32) + p["b_f"].astype(jnp.float32))
    if past is None:
        k_all, v_all, lf_all, offset = k, v, logf, 0
    else:
        k_past, v_past, lf_past = past
        offset = k_past.shape[1]
        k_all = jnp.concatenate([k_past.astype(k.dtype), k], axis=1)
        v_all = jnp.concatenate([v_past.astype(v.dtype), v], axis=1)
        lf_all = jnp.concatenate([lf_past.astype(jnp.float32), logf], axis=1)
    c = jnp.cumsum(lf_all, axis=1)
    attn = fox_attend(q, k_all, v_all, c[:, offset:], c, offset)
    s, hT_re, hT_im = s5_scan(u, h0_re, h0_im, p)
    zs = jax.nn.gelu(s)
    ssm_out = (zs @ p["w_glu_a"]) * jax.nn.sigmoid(zs @ p["w_glu_b"])
    attn_out = attn @ p["w_attn_out"]
    merged = jax.nn.sigmoid(gate_s) * ssm_out + jax.nn.sigmoid(gate_a) * attn_out
    x = x + merged @ p["w_out"]
    h2 = rmsnorm(x, p["g_ffn"])
    gc, conv_new = causal_dwconv(h2 @ p["w_gate"], conv_prev, p["w_conv"], p["b_conv"])
    x = x + (jax.nn.gelu(gc) * (h2 @ p["w_up"])) @ p["w_down"]
    return x, (k, v, logf, hT_re, hT_im, conv_new)


def setup_inputs(seed: int = 0) -> dict:
    key = jax.random.key(seed)
    ks = iter(jax.random.split(key, 40))
    nrm = lambda shape, s=1.0: jax.random.normal(next(ks), shape, jnp.float32) * s
    n_pages = PAST_LEN // PAGE_SIZE
    in_use = DEC_BATCH * n_pages
    n_pool = in_use + max(1, in_use // 4)
    page_table = jax.random.permutation(next(ks), n_pool)[:in_use].reshape(DEC_BATCH, n_pages).astype(jnp.int32)
    lam_im = jnp.broadcast_to(math.pi * jnp.arange(STATE_P, dtype=jnp.float32), (DEPTH, N_GROUPS, STATE_P))
    return {
        "x_prompt": nrm((BATCH, SEQ, D_MODEL)),
        "x_sample": nrm((DEC_BATCH, DEC_SEQ, D_MODEL)),
        "cache_k": nrm((DEPTH, n_pool, PAGE_SIZE, N_HEADS, HEAD_DIM)),
        "cache_v": nrm((DEPTH, n_pool, PAGE_SIZE, N_HEADS, HEAD_DIM)),
        "cache_logf": jax.nn.log_sigmoid(FORGET_BIAS + nrm((DEPTH, n_pool, PAGE_SIZE, N_HEADS))),
        "page_table": page_table,
        "state_ssm_re": nrm((DEPTH, DEC_BATCH, N_GROUPS, STATE_P), 0.3),
        "state_ssm_im": nrm((DEPTH, DEC_BATCH, N_GROUPS, STATE_P), 0.3),
        "state_conv": nrm((DEPTH, DEC_BATCH, CONV_W - 1, D_FF)),
        "g_mix": 1.0 + nrm((DEPTH, D_MODEL), 0.02),
        "w_in": nrm((DEPTH, D_MODEL, D_IN), D_MODEL ** -0.5),
        "b_f": FORGET_BIAS + nrm((DEPTH, N_HEADS), 0.1),
        "lam_re": -0.5 + nrm((DEPTH, N_GROUPS, STATE_P), 0.01),
        "lam_im": lam_im + nrm((DEPTH, N_GROUPS, STATE_P), 0.01),
        "log_dt": jax.random.uniform(next(ks), (DEPTH, N_GROUPS), jnp.float32, math.log(DT_MIN), math.log(DT_MAX)),
        "b_re": nrm((DEPTH, N_GROUPS, STATE_P, SSM_GROUP), SSM_GROUP ** -0.5),
        "b_im": nrm((DEPTH, N_GROUPS, STATE_P, SSM_GROUP), SSM_GROUP ** -0.5),
        "c_re": nrm((DEPTH, N_GROUPS, SSM_GROUP, STATE_P), STATE_P ** -0.5),
        "c_im": nrm((DEPTH, N_GROUPS, SSM_GROUP, STATE_P), STATE_P ** -0.5),
        "d_skip": nrm((DEPTH, D_SSM)),
        "w_glu_a": nrm((DEPTH, D_SSM, D_MODEL), D_SSM ** -0.5),
        "w_glu_b": nrm((DEPTH, D_SSM, D_MODEL), D_SSM ** -0.5),
        "w_attn_out": nrm((DEPTH, D_ATTN, D_MODEL), D_ATTN ** -0.5),
        "w_out": nrm((DEPTH, D_MODEL, D_MODEL), D_MODEL ** -0.5),
        "g_ffn": 1.0 + nrm((DEPTH, D_MODEL), 0.02),
        "w_gate": nrm((DEPTH, D_MODEL, D_FF), D_MODEL ** -0.5),
        "w_up": nrm((DEPTH, D_MODEL, D_FF), D_MODEL ** -0.5),
        "w_conv": nrm((DEPTH, CONV_W, D_FF), CONV_W ** -0.5),
        "b_conv": nrm((DEPTH, D_FF), 0.01),
        "w_down": nrm((DEPTH, D_FF, D_MODEL), D_FF ** -0.5),
        "g_final": 1.0 + nrm((D_MODEL,), 0.02),
    }


def reference(x_prompt, x_sample, cache_k, cache_v, cache_logf, page_table, state_ssm_re, state_ssm_im, state_conv,
              g_mix, w_in, b_f, lam_re, lam_im, log_dt, b_re, b_im, c_re, c_im, d_skip, w_glu_a, w_glu_b,
              w_attn_out, w_out, g_ffn, w_gate, w_up, w_conv, b_conv, w_down, g_final):
    Bp = x_prompt.shape[0]
    Bd = x_sample.shape[0]
    L = page_table.shape[1] * PAGE_SIZE
    xp, xs = x_prompt, x_sample
    outs_p, outs_s = [], []
    for l in range(DEPTH):
        p = {"g_mix": g_mix[l], "w_in": w_in[l], "b_f": b_f[l], "lam_re": lam_re[l], "lam_im": lam_im[l],
             "log_dt": log_dt[l], "b_re": b_re[l], "b_im": b_im[l], "c_re": c_re[l], "c_im": c_im[l],
             "d_skip": d_skip[l], "w_glu_a": w_glu_a[l], "w_glu_b": w_glu_b[l], "w_attn_out": w_attn_out[l],
             "w_out": w_out[l], "g_ffn": g_ffn[l], "w_gate": w_gate[l], "w_up": w_up[l], "w_conv": w_conv[l],
             "b_conv": b_conv[l], "w_down": w_down[l]}
        zeros_h = jnp.zeros((Bp, N_GROUPS, STATE_P), jnp.float32)
        zeros_c = jnp.zeros((Bp, CONV_W - 1, D_FF), xp.dtype)
        xp, st_p = trunk_layer(xp, None, zeros_h, zeros_h, zeros_c, p)
        past = (cache_k[l][page_table].reshape(Bd, L, N_HEADS, HEAD_DIM),
                cache_v[l][page_table].reshape(Bd, L, N_HEADS, HEAD_DIM),
                cache_logf[l][page_table].reshape(Bd, L, N_HEADS))
        xs, st_s = trunk_layer(xs, past, state_ssm_re[l], state_ssm_im[l], state_conv[l], p)
        outs_p.append(st_p)
        outs_s.append(st_s)
    y_prompt = rmsnorm(xp, g_final)
    y_sample = rmsnorm(xs, g_final)
    k_prompt = jnp.stack([o[0] for o in outs_p])
    v_prompt = jnp.stack([o[1] for o in outs_p])
    logf_prompt = jnp.stack([o[2] for o in outs_p])
    ssm_re_prompt = jnp.stack([o[3] for o in outs_p])
    ssm_im_prompt = jnp.stack([o[4] for o in outs_p])
    conv_prompt = jnp.stack([o[5] for o in outs_p])
    k_sample = jnp.stack([o[0] for o in outs_s])
    v_sample = jnp.stack([o[1] for o in outs_s])
    logf_sample = jnp.stack([o[2] for o in outs_s])
    ssm_re_sample = jnp.stack([o[3] for o in outs_s])
    ssm_im_sample = jnp.stack([o[4] for o in outs_s])
    conv_sample = jnp.stack([o[5] for o in outs_s])
    return (y_prompt, y_sample, k_prompt, v_prompt, logf_prompt, k_sample, v_sample, logf_sample,
            ssm_re_prompt, ssm_im_prompt, ssm_re_sample, ssm_im_sample, conv_prompt, conv_sample)
```

```python
import functools

import jax
import jax.numpy as jnp
from jax import lax
from jax.experimental import pallas as pl
from jax.experimental.pallas import tpu as pltpu

N_HEADS = 8
HEAD_DIM = 64
D_ATTN = N_HEADS * HEAD_DIM
SSM_GROUP = 16
STATE_P = 64
CONV_W = 3
PAGE_SIZE = 128
RMS_EPS = 1e-6
NEG_INF = -1e30
SCALE = HEAD_DIM ** -0.5

LANES = 128
S5_BLOCK = 16
VMEM_LIMIT = 56 * 1024 * 1024

F32 = jnp.float32
BF16 = jnp.bfloat16
HI = lax.Precision.HIGHEST


def _dot(a, b):
    return jnp.dot(a, b, preferred_element_type=F32)


def _dot_nt(a, b):
    return lax.dot_general(a, b, (((1,), (1,)), ((), ())), preferred_element_type=F32)


def _split3(x):
    hi = x.astype(BF16)
    r1 = x - hi.astype(F32)
    mid = r1.astype(BF16)
    lo = (r1 - mid.astype(F32)).astype(BF16)
    return hi, mid, lo


def _rmsnorm(x, g):
    return x * lax.rsqrt(jnp.mean(x * x, axis=-1, keepdims=True) + RMS_EPS) * g


def _pick_tile(n, pref):
    t = min(n, pref)
    while n % t:
        t //= 2
    return t


def _const_spec(shape):
    nd = len(shape)
    return pl.BlockSpec(shape, lambda *_: (0,) * nd, pipeline_mode=pl.Buffered(1))


def _params(sem):
    return pltpu.CompilerParams(dimension_semantics=sem, vmem_limit_bytes=VMEM_LIMIT)


def _in_proj_kernel(*refs, prompt, tm):
    if prompt:
        (x_ref, g_ref, wqkv_ref, wf_ref, bf_ref, wu_ref, wga_ref, wgs_ref, selq_ref, selk_ref, oneq_ref, onek_ref,
         k_ref, v_ref, lf_ref, u_ref, sga_ref, sgs_ref, qa_ref, ka_ref, va_ref, carry_ref) = refs
    else:
        (x_ref, g_ref, wqkv_ref, wf_ref, bf_ref, wu_ref, wga_ref, wgs_ref,
         q_ref, k_ref, v_ref, lf_ref, u_ref, sga_ref, sgs_ref) = refs

    hb = _rmsnorm(x_ref[...], g_ref[...]).astype(BF16)
    qkv = _dot(hb, wqkv_ref[...])
    q = qkv[:, :D_ATTN]
    k = qkv[:, D_ATTN:2 * D_ATTN]
    v = qkv[:, 2 * D_ATTN:]
    k_ref[...] = k
    v_ref[...] = v
    logf = jax.nn.log_sigmoid(_dot(hb, wf_ref[...]) + bf_ref[...])
    lf_ref[...] = logf[:, :N_HEADS]
    u_ref[...] = _dot(hb, wu_ref[...])
    sga_ref[...] = jax.nn.sigmoid(_dot(hb, wga_ref[...]))
    sgs_ref[...] = jax.nn.sigmoid(_dot(hb, wgs_ref[...]))
    if not prompt:
        q_ref[...] = q
        return

    @pl.when(pl.program_id(1) == 0)
    def _():
        carry_ref[...] = jnp.zeros_like(carry_ref)

    row = lax.broadcasted_iota(jnp.int32, (tm, tm), 0)
    col = lax.broadcasted_iota(jnp.int32, (tm, tm), 1)
    tri = jnp.where(row >= col, 1.0, 0.0).astype(BF16)
    l_hi, l_mid, l_lo = _split3(logf)
    c = _dot(tri, l_hi) + _dot(tri, l_mid) + _dot(tri, l_lo) + carry_ref[...]
    carry_ref[...] = c[tm - 1:tm, :]

    e = jnp.concatenate(_split3(c), axis=1)
    eq = _dot(e, selq_ref[...]) + oneq_ref[...]
    ek = _dot(e, selk_ref[...]) + onek_ref[...]
    lane = lax.broadcasted_iota(jnp.int32, (tm, LANES), 1)
    low = lane < HEAD_DIM
    for h in range(N_HEADS):
        t = h // 2
        qt = q[:, t * LANES:(t + 1) * LANES]
        kt = k[:, t * LANES:(t + 1) * LANES]
        if h % 2:
            qt = pltpu.roll(qt, HEAD_DIM, 1)
            kt = pltpu.roll(kt, HEAD_DIM, 1)
        qa_ref[h] = jnp.where(low, qt, eq[:, h * LANES:(h + 1) * LANES]).astype(BF16)
        ka_ref[h] = jnp.where(low, kt, ek[:, h * LANES:(h + 1) * LANES]).astype(BF16)
    ones_tile = jnp.where(lane == 0, 1.0, 0.0)
    for p in range(N_HEADS // 2):
        va_ref[p] = jnp.concatenate([v[:, p * LANES:(p + 1) * LANES], ones_tile], axis=1).astype(BF16)


def _in_proj(x, g_mix, w, *, prompt):
    B, T, D = x.shape
    tm = _pick_tile(T, 256)
    nt = T // tm
    d_gate = w["wga"].shape[1]
    row_spec = lambda n: pl.BlockSpec((None, tm, n), lambda b, t: (b, t, 0))
    ins = [x, g_mix, w["wqkv"], w["wf"], w["bf"], w["wu"], w["wga"], w["wgs"]]
    in_specs = [row_spec(D)] + [_const_spec(a.shape) for a in ins[1:]]
    outs = [jax.ShapeDtypeStruct((B, T, D_ATTN), F32),
            jax.ShapeDtypeStruct((B, T, D_ATTN), F32),
            jax.ShapeDtypeStruct((B, T, N_HEADS), F32),
            jax.ShapeDtypeStruct((B, T, w["wu"].shape[1]), F32),
            jax.ShapeDtypeStruct((B, T, d_gate), F32),
            jax.ShapeDtypeStruct((B, T, d_gate), F32)]
    out_specs = [row_spec(D_ATTN), row_spec(D_ATTN), row_spec(N_HEADS), row_spec(w["wu"].shape[1]),
                 row_spec(d_gate), row_spec(d_gate)]
    scratch = []
    if prompt:
        extra = [w["selq"], w["selk"], w["oneq"], w["onek"]]
        ins += extra
        in_specs += [_const_spec(a.shape) for a in extra]
        head_spec = lambda nh, n: pl.BlockSpec((None, nh, tm, n), lambda b, t: (b, 0, t, 0))
        outs += [jax.ShapeDtypeStruct((B, N_HEADS, T, LANES), BF16),
                 jax.ShapeDtypeStruct((B, N_HEADS, T, LANES), BF16),
                 jax.ShapeDtypeStruct((B, N_HEADS // 2, T, 2 * LANES), BF16)]
        out_specs += [head_spec(N_HEADS, LANES), head_spec(N_HEADS, LANES), head_spec(N_HEADS // 2, 2 * LANES)]
        scratch = [pltpu.VMEM((1, LANES), F32)]
    else:
        outs = [jax.ShapeDtypeStruct((B, T, D_ATTN), F32)] + outs
        out_specs = [row_spec(D_ATTN)] + out_specs
    return pl.pallas_call(
        functools.partial(_in_proj_kernel, prompt=prompt, tm=tm),
        grid=(B, nt), in_specs=in_specs, out_specs=out_specs, out_shape=outs, scratch_shapes=scratch,
        compiler_params=_params(("arbitrary", "arbitrary")),
        name="in_proj_prompt" if prompt else "in_proj_sample",
    )(*ins)


def _attn_prompt_kernel(qa_ref, ka_ref, va_ref, o_ref, acc_ref, m_ref, *, blk):
    qi = pl.program_id(2)
    lane = lax.broadcasted_iota(jnp.int32, (blk, LANES), 1)
    row = lax.broadcasted_iota(jnp.int32, (blk, blk), 0)
    col = lax.broadcasted_iota(jnp.int32, (blk, blk), 1)
    outs = []
    for hh in range(2):
        q = qa_ref[hh]
        acc_ref[...] = jnp.zeros_like(acc_ref)
        m_ref[...] = jnp.full_like(m_ref, NEG_INF)

        def update(j, masked):
            start = pl.multiple_of(j * blk, blk)
            s = _dot_nt(q, ka_ref[hh, pl.ds(start, blk), :])
            if masked:
                s = jnp.where(col <= row, s, NEG_INF)
            m_prev = m_ref[...]
            m_new = jnp.maximum(m_prev, jnp.max(s, axis=1, keepdims=True))
            p = jnp.exp(s - m_new).astype(BF16)
            acc_ref[...] = jnp.exp(m_prev - m_new) * acc_ref[...] + _dot(p, va_ref[pl.ds(start, blk), :])
            m_ref[...] = m_new

        def body(j, carry):
            update(j, False)
            return carry

        lax.fori_loop(0, qi, body, 0)
        update(qi, True)
        acc = acc_ref[...]
        outs.append(acc[:, :LANES] / acc[:, LANES:LANES + 1])
    o_ref[...] = jnp.where(lane < HEAD_DIM, outs[0], outs[1]).astype(o_ref.dtype)


def _attn_prompt(qa, ka, va):
    B, H, T, _ = qa.shape
    blk = _pick_tile(T, 512)
    nq = T // blk
    return pl.pallas_call(
        functools.partial(_attn_prompt_kernel, blk=blk),
        grid=(B, H // 2, nq),
        in_specs=[pl.BlockSpec((None, 2, blk, LANES), lambda b, hp, qi: (b, hp, qi, 0)),
                  pl.BlockSpec((None, 2, T, LANES), lambda b, hp, qi: (b, hp, 0, 0)),
                  pl.BlockSpec((None, None, T, 2 * LANES), lambda b, hp, qi: (b, hp, 0, 0))],
        out_specs=pl.BlockSpec((None, blk, LANES), lambda b, hp, qi: (b, qi, hp)),
        out_shape=jax.ShapeDtypeStruct((B, T, D_ATTN), BF16),
        scratch_shapes=[pltpu.VMEM((blk, 2 * LANES), F32), pltpu.VMEM((blk, 1), F32)],
        compiler_params=_params(("arbitrary", "arbitrary", "arbitrary")),
        name="attn_prompt",
    )(qa, ka, va)


def _attn_sample_kernel(pt_ref, q_ref, kn_ref, vn_ref, lfn_ref, *rest, pg, tq):
    k_refs = rest[:pg]
    v_refs = rest[pg:2 * pg]
    lf_refs = rest[2 * pg:3 * pg]
    o_ref, qbd_ref, acc_ref, m_ref, l_ref, coff_ref = rest[3 * pg:]
    del pt_ref
    j = pl.program_id(1)
    hq = N_HEADS * tq

    @pl.when(j == 0)
    def _():
        q = q_ref[...]
        lane_head = lax.broadcasted_iota(jnp.int32, (tq, D_ATTN), 1) // HEAD_DIM
        qbd_ref[...] = jnp.concatenate(
            [jnp.where(lane_head == h, q, 0.0) for h in range(N_HEADS)], axis=0).astype(BF16)
        acc_ref[...] = jnp.zeros_like(acc_ref)
        l_ref[...] = jnp.zeros_like(l_ref)
        m_ref[...] = jnp.full_like(m_ref, NEG_INF)
        coff_ref[...] = jnp.zeros_like(coff_ref)

    r = lax.broadcasted_iota(jnp.int32, (PAGE_SIZE, PAGE_SIZE), 0)
    cc = lax.broadcasted_iota(jnp.int32, (PAGE_SIZE, PAGE_SIZE), 1)
    triu = jnp.where(r <= cc, 1.0, 0.0).astype(BF16)
    qbd = qbd_ref[...]

    def page_logits(k_page, lft, coff):
        s = _dot_nt(qbd, k_page.astype(BF16))
        a, b, c3 = _split3(lft)
        ck = _dot(a, triu) + _dot(b, triu) + _dot(c3, triu) + coff
        bias = jnp.concatenate(
            [jnp.broadcast_to(ck[h:h + 1, :], (tq, PAGE_SIZE)) for h in range(N_HEADS)], axis=0)
        return s - bias, ck[:, PAGE_SIZE - 1:PAGE_SIZE]

    def online_update(s_list, v_list):
        s = jnp.concatenate(s_list, axis=1) if len(s_list) > 1 else s_list[0]
        m_prev = m_ref[...]
        m_new = jnp.maximum(m_prev, jnp.max(s, axis=1, keepdims=True))
        p = jnp.exp(s - m_new)
        alpha = jnp.exp(m_prev - m_new)
        pv = _dot(p[:, :PAGE_SIZE].astype(BF16), v_list[0].astype(BF16))
        for i in range(1, len(v_list)):
            pv += _dot(p[:, i * PAGE_SIZE:(i + 1) * PAGE_SIZE].astype(BF16), v_list[i].astype(BF16))
        acc_ref[...] = alpha * acc_ref[...] + pv
        l_ref[...] = alpha * l_ref[...] + jnp.sum(p, axis=1, keepdims=True)
        m_ref[...] = m_new

    coff = coff_ref[...]
    s_list = []
    for i in range(pg):
        s, coff = page_logits(k_refs[i][...], lf_refs[i][...], coff)
        s_list.append(s)
    coff_ref[...] = coff
    online_update(s_list, [v_refs[i][...] for i in range(pg)])

    @pl.when(j == pl.num_programs(1) - 1)
    def _():
        s, _unused = page_logits(kn_ref[...], lfn_ref[...], coff_ref[...])
        key = lax.broadcasted_iota(jnp.int32, (hq, PAGE_SIZE), 1)
        qtok = lax.broadcasted_iota(jnp.int32, (hq, PAGE_SIZE), 0) % tq
        s = jnp.where(key <= qtok, s, NEG_INF)
        online_update([s], [vn_ref[...]])
        o = acc_ref[...] / l_ref[...]
        lane = lax.broadcasted_iota(jnp.int32, (tq, LANES), 1)
        tiles = []
        for t in range(N_HEADS // 2):
            a = o[(2 * t) * tq:(2 * t + 1) * tq, t * LANES:(t + 1) * LANES]
            b = o[(2 * t + 1) * tq:(2 * t + 2) * tq, t * LANES:(t + 1) * LANES]
            tiles.append(jnp.where(lane < HEAD_DIM, a, b))
        o_ref[...] = jnp.concatenate(tiles, axis=1)


def _attn_sample(page_table, q, k_new, v_new, lf_new_t, cache_k, cache_v, cache_lf_t):
    Bd, tq, _ = q.shape
    n_pages = page_table.shape[1]
    pg = _pick_tile(n_pages, 8)
    ng = n_pages // pg
    pt_flat = page_table.reshape(-1)

    def page_spec(shape, i):
        return pl.BlockSpec((None,) + shape, lambda b, j, pt: (pt[b * n_pages + j * pg + i], 0, 0))

    seq_spec = lambda shape: pl.BlockSpec((None,) + shape, lambda b, j, pt: (b, 0, 0))
    in_specs = ([seq_spec((tq, D_ATTN)), seq_spec((PAGE_SIZE, D_ATTN)), seq_spec((PAGE_SIZE, D_ATTN)),
                 seq_spec((N_HEADS, PAGE_SIZE))]
                + [page_spec((PAGE_SIZE, D_ATTN), i) for i in range(pg)]
                + [page_spec((PAGE_SIZE, D_ATTN), i) for i in range(pg)]
                + [page_spec((N_HEADS, PAGE_SIZE), i) for i in range(pg)])
    hq = N_HEADS * tq
    grid_spec = pltpu.PrefetchScalarGridSpec(
        num_scalar_prefetch=1, grid=(Bd, ng), in_specs=in_specs,
        out_specs=pl.BlockSpec((None, tq, D_ATTN), lambda b, j, pt: (b, 0, 0)),
        scratch_shapes=[pltpu.VMEM((hq, D_ATTN), BF16), pltpu.VMEM((hq, D_ATTN), F32),
                        pltpu.VMEM((hq, 1), F32), pltpu.VMEM((hq, 1), F32), pltpu.VMEM((N_HEADS, 1), F32)])
    return pl.pallas_call(
        functools.partial(_attn_sample_kernel, pg=pg, tq=tq),
        grid_spec=grid_spec,
        out_shape=jax.ShapeDtypeStruct((Bd, tq, D_ATTN), F32),
        compiler_params=_params(("arbitrary", "arbitrary")),
        name="attn_sample",
    )(pt_flat, q, k_new, v_new, lf_new_t, *([cache_k] * pg), *([cache_v] * pg), *([cache_lf_t] * pg))


def _s5_discretize(p):
    lr, li = p["lam_re"], p["lam_im"]
    dt = jnp.exp(p["log_dt"])[:, None]
    mag = jnp.exp(lr * dt)
    ang = li * dt
    ab_re = mag * jnp.cos(ang)
    ab_im = mag * jnp.sin(ang)
    den = lr * lr + li * li
    n_re = ab_re - 1.0
    n_im = ab_im
    f_re = (n_re * lr + n_im * li) / den
    f_im = (n_im * lr - n_re * li) / den
    bb_re = f_re[..., None] * p["b_re"] - f_im[..., None] * p["b_im"]
    bb_im = f_re[..., None] * p["b_im"] + f_im[..., None] * p["b_re"]
    return ab_re, ab_im, bb_re, bb_im


def _s5_block_weights(p):
    R = S5_BLOCK
    ab_re, ab_im, bb_re, bb_im = _s5_discretize(p)
    G, P, C = bb_re.shape
    pw_re, pw_im = [jnp.ones_like(ab_re)], [jnp.zeros_like(ab_im)]
    for _ in range(R):
        pr, pi = pw_re[-1], pw_im[-1]
        pw_re.append(pr * ab_re - pi * ab_im)
        pw_im.append(pr * ab_im + pi * ab_re)
    pw_re = jnp.stack(pw_re)
    pw_im = jnp.stack(pw_im)
    c_re, c_im = p["c_re"], p["c_im"]
    cp_re = c_re[None] * pw_re[:, :, None, :] - c_im[None] * pw_im[:, :, None, :]
    cp_im = c_re[None] * pw_im[:, :, None, :] + c_im[None] * pw_re[:, :, None, :]
    m = (jnp.einsum('jgcp,gpd->jgdc', cp_re[:R], bb_re, precision=HI)
         - jnp.einsum('jgcp,gpd->jgdc', cp_im[:R], bb_im, precision=HI))
    lag = jnp.arange(R)[None, :] - jnp.arange(R)[:, None]
    kt = jnp.where((lag >= 0)[:, :, None, None, None], m[jnp.clip(lag, 0, R - 1)], 0.0)
    kt = kt.transpose(2, 0, 3, 1, 4).reshape(G, R * C, R * C)
    pb_re = pw_re[R - 1::-1][:R]
    pb_im = pw_im[R - 1::-1][:R]
    x_re = pb_re[..., None] * bb_re[None] - pb_im[..., None] * bb_im[None]
    x_im = pb_re[..., None] * bb_im[None] + pb_im[..., None] * bb_re[None]
    x_re = x_re.transpose(1, 0, 3, 2).reshape(G, R * C, P)
    x_im = x_im.transpose(1, 0, 3, 2).reshape(G, R * C, P)
    wx = jnp.concatenate([x_re, x_im, x_im, x_re], axis=-1)
    wc_re = cp_re[1:].transpose(1, 3, 0, 2).reshape(G, P, R * C)
    wc_im = cp_im[1:].transpose(1, 3, 0, 2).reshape(G, P, R * C)
    wc = jnp.concatenate([wc_re, -wc_im], axis=1)
    a_blk = jnp.concatenate([pw_re[R], pw_re[R]], axis=-1)
    b_blk = jnp.concatenate([-pw_im[R], pw_im[R]], axis=-1)
    d = jnp.tile(p["d_skip"].reshape(G, 1, C), (1, 1, R))
    return kt.astype(BF16), wx.astype(BF16), wc.astype(BF16), a_blk, b_blk, d


def _s5_prompt_kernel(u_ref, kt_ref, wx_ref, wc_ref, a_ref, b_ref, d_ref, y_ref, hfin_ref,
                      xa_ref, xb_ref, hp_ref, h_ref, hsw_ref, *, tb, ng):
    @pl.when(pl.program_id(1) == 0)
    def _():
        h_ref[...] = jnp.zeros_like(h_ref)
        hsw_ref[...] = jnp.zeros_like(hsw_ref)

    sp = 2 * STATE_P

    def xbody(g, carry):
        x = _dot(u_ref[g].astype(BF16), wx_ref[g])
        rows = pl.ds(pl.multiple_of(g * tb, tb), tb)
        xa_ref[rows, :] = x[:, :sp]
        xb_ref[rows, :] = x[:, sp:]
        return carry

    lax.fori_loop(0, ng, xbody, 0)

    a = a_ref[...]
    b = b_ref[...]

    def sbody(k, carry):
        h, hsw = carry
        rows = pl.ds(k, ng, stride=tb)
        hp_ref[rows, :] = h
        return a * h + b * hsw + xa_ref[rows, :], a * hsw - b * h + xb_ref[rows, :]

    h, hsw = lax.fori_loop(0, tb, sbody, (h_ref[...], hsw_ref[...]))
    h_ref[...] = h
    hsw_ref[...] = hsw
    hfin_ref[...] = h

    def ybody(g, carry):
        u = u_ref[g]
        hp = hp_ref[pl.ds(pl.multiple_of(g * tb, tb), tb), :].astype(BF16)
        y_ref[g] = _dot(u.astype(BF16), kt_ref[g]) + _dot(hp, wc_ref[g]) + u * d_ref[g]
        return carry

    lax.fori_loop(0, ng, ybody, 0)


def _s5_prompt(ug, weights):
    kt, wx, wc, a_blk, b_blk, d = weights
    B, G, nb, w = ug.shape
    tb = _pick_tile(nb, 128)
    sp = 2 * STATE_P
    blk_spec = pl.BlockSpec((None, G, tb, w), lambda b, t: (b, 0, t, 0))
    return pl.pallas_call(
        functools.partial(_s5_prompt_kernel, tb=tb, ng=G),
        grid=(B, nb // tb),
        in_specs=[blk_spec] + [_const_spec(x.shape) for x in (kt, wx, wc, a_blk, b_blk, d)],
        out_specs=[blk_spec, pl.BlockSpec((None, G, sp), lambda b, t: (b, 0, 0))],
        out_shape=[jax.ShapeDtypeStruct(ug.shape, F32), jax.ShapeDtypeStruct((B, G, sp), F32)],
        scratch_shapes=[pltpu.VMEM((G * tb, sp), F32), pltpu.VMEM((G * tb, sp), F32), pltpu.VMEM((G * tb, sp), F32),
                        pltpu.VMEM((G, sp), F32), pltpu.VMEM((G, sp), F32)],
        compiler_params=_params(("arbitrary", "arbitrary")),
        name="s5_prompt",
    )(ug, kt, wx, wc, a_blk, b_blk, d)


def _s5_sample_kernel(u_ref, h0_ref, bbd_ref, cmat_ref, are_ref, aim_ref, d_ref, y_ref, ht_ref, hs_ref, *, nt, nb):
    ns = are_ref.shape[1]
    u = u_ref[...]
    ub = u.astype(BF16)
    cw = 256
    for c0 in range(0, ns, cw):
        x_re = _dot(ub, bbd_ref[:, c0:c0 + cw])
        x_im = _dot(ub, bbd_ref[:, ns + c0:ns + c0 + cw])
        are = are_ref[:, c0:c0 + cw]
        aim = aim_ref[:, c0:c0 + cw]
        hr = h0_ref[:, c0:c0 + cw]
        hi = h0_ref[:, ns + c0:ns + c0 + cw]
        for t in range(nt):
            rows = slice(t * nb, (t + 1) * nb)
            hr, hi = are * hr - aim * hi + x_re[rows], are * hi + aim * hr + x_im[rows]
            hs_ref[rows, c0:c0 + cw] = hr.astype(BF16)
            hs_ref[rows, ns + c0:ns + c0 + cw] = hi.astype(BF16)
        ht_ref[:, c0:c0 + cw] = hr
        ht_ref[:, ns + c0:ns + c0 + cw] = hi
    y_ref[...] = _dot(hs_ref[...], cmat_ref[...]) + u * d_ref[...]


def _s5_sample(u_tm, h0, p, nt):
    ab_re, ab_im, bb_re, bb_im = _s5_discretize(p)
    G, P, C = bb_re.shape
    eye = jnp.eye(G, dtype=F32)
    bd = lambda w: jnp.einsum('gpc,gk->gckp', w, eye).reshape(G * C, G * P)
    bbd = jnp.concatenate([bd(bb_re), bd(bb_im)], axis=1).astype(BF16)
    cd = lambda w: jnp.einsum('gcp,gk->gpkc', w, eye).reshape(G * P, G * C)
    cmat = jnp.concatenate([cd(p["c_re"]), -cd(p["c_im"])], axis=0).astype(BF16)
    nb = h0.shape[0]
    ins = [u_tm, h0, bbd, cmat, ab_re.reshape(1, G * P), ab_im.reshape(1, G * P), p["d_skip"].reshape(1, G * C)]
    return pl.pallas_call(
        functools.partial(_s5_sample_kernel, nt=nt, nb=nb),
        grid=(1,),
        in_specs=[_const_spec(x.shape) for x in ins],
        out_specs=[pl.BlockSpec(u_tm.shape, lambda i: (0, 0)), pl.BlockSpec(h0.shape, lambda i: (0, 0))],
        out_shape=[jax.ShapeDtypeStruct(u_tm.shape, F32), jax.ShapeDtypeStruct(h0.shape, F32)],
        scratch_shapes=[pltpu.VMEM((nt * nb, 2 * G * P), BF16)],
        compiler_params=_params(("arbitrary",)),
        name="s5_sample",
    )(*ins)


def _mix_ffn_kernel(x_ref, s_ref, attn_ref, sga_ref, sgs_ref, prev_ref,
                    wa_ref, wb_ref, wao_ref, wout_ref, gffn_ref, wg_ref, wup_ref, wconv_ref, bconv_ref, wd_ref,
                    gfin_ref, y_ref, conv_ref, gate_ref, *, tm, shift, pad):
    hist = (CONV_W - 1) * shift

    @pl.when(pl.program_id(1) == 0)
    def _():
        gate_ref[pad - hist:pad, :] = prev_ref[...]

    zb = jax.nn.gelu(s_ref[...]).astype(BF16)
    ssm_out = _dot(zb, wa_ref[...]) * jax.nn.sigmoid(_dot(zb, wb_ref[...]))
    attn_out = _dot(attn_ref[...], wao_ref[...])
    merged = sgs_ref[...] * ssm_out + sga_ref[...] * attn_out
    x1 = x_ref[...] + _dot(merged.astype(BF16), wout_ref[...])
    h2 = _rmsnorm(x1, gffn_ref[...]).astype(BF16)
    gate_ref[pad:pad + tm, :] = _dot(h2, wg_ref[...])
    gc = gate_ref[pad - hist:pad - hist + tm, :] * wconv_ref[0:1, :]
    for i in range(1, CONV_W):
        gc = gc + gate_ref[pad - hist + i * shift:pad - hist + i * shift + tm, :] * wconv_ref[i:i + 1, :]
    gc = bconv_ref[...] + gc
    act = (jax.nn.gelu(gc) * _dot(h2, wup_ref[...])).astype(BF16)
    x2 = x1 + _dot(act, wd_ref[...])
    y_ref[...] = _rmsnorm(x2, gfin_ref[...])
    tail = gate_ref[pad + tm - hist:pad + tm, :]
    conv_ref[...] = tail
    gate_ref[pad - hist:pad, :] = tail


def _mix_ffn(x, s, attn, sga, sgs, prev, w, *, shift, tm_pref):
    B, T, D = x.shape
    tm = _pick_tile(T, tm_pref)
    hist = (CONV_W - 1) * shift
    pad = -(-hist // 8) * 8
    F = w["wg"].shape[1]
    row_spec = lambda n: pl.BlockSpec((None, tm, n), lambda b, t: (b, t, 0))
    seq_spec = pl.BlockSpec((None, hist, F), lambda b, t: (b, 0, 0))
    consts = [w["wa"], w["wb"], w["wao"], w["wout"], w["gffn"], w["wg"], w["wup"], w["wconv"], w["bconv"], w["wd"],
              w["gfin"]]
    return pl.pallas_call(
        functools.partial(_mix_ffn_kernel, tm=tm, shift=shift, pad=pad),
        grid=(B, T // tm),
        in_specs=[row_spec(D), row_spec(s.shape[2]), row_spec(attn.shape[2]), row_spec(D), row_spec(D), seq_spec]
        + [_const_spec(a.shape) for a in consts],
        out_specs=[row_spec(D), seq_spec],
        out_shape=[jax.ShapeDtypeStruct((B, T, D), F32), jax.ShapeDtypeStruct((B, hist, F), F32)],
        scratch_shapes=[pltpu.VMEM((pad + tm, F), F32)],
        compiler_params=_params(("arbitrary", "arbitrary")),
        name="mix_ffn_s%d" % shift,
    )(x, s, attn, sga, sgs, prev, *consts)


def _pack_weights(l, g_mix, w_in, b_f, w_glu_a, w_glu_b, w_attn_out, w_out, g_ffn, w_gate, w_up, w_conv, b_conv,
                  w_down, g_final):
    D = w_in.shape[1]
    d_ssm = w_glu_a.shape[1]
    o = [0, D_ATTN, 2 * D_ATTN, 3 * D_ATTN, 3 * D_ATTN + N_HEADS, 3 * D_ATTN + N_HEADS + d_ssm]
    wi = w_in[l]
    wq = wi[:, o[0]:o[1]] * SCALE
    w = {
        "wqkv": jnp.concatenate([wq, wi[:, o[1]:o[3]]], axis=1).astype(BF16),
        "wf": jnp.pad(wi[:, o[3]:o[4]], ((0, 0), (0, LANES - N_HEADS))).astype(BF16),
        "bf": jnp.pad(b_f[l], (0, LANES - N_HEADS)).reshape(1, LANES),
        "wu": wi[:, o[4]:o[5]].astype(BF16),
        "wga": wi[:, o[5]:o[5] + D].astype(BF16),
        "wgs": wi[:, o[5] + D:o[5] + 2 * D].astype(BF16),
        "wa": w_glu_a[l].astype(BF16), "wb": w_glu_b[l].astype(BF16), "wao": w_attn_out[l].astype(BF16),
        "wout": w_out[l].astype(BF16), "gffn": g_ffn[l].reshape(1, D), "wg": w_gate[l].astype(BF16),
        "wup": w_up[l].astype(BF16), "wconv": w_conv[l], "bconv": b_conv[l].reshape(1, -1),
        "wd": w_down[l].astype(BF16), "gfin": g_final.reshape(1, D),
    }
    e = jnp.arange(3 * LANES)
    part, head = e // LANES, e % LANES
    colq = head * LANES + HEAD_DIM + part
    colk = head * LANES + HEAD_DIM + 3 + part
    valid = (head < N_HEADS)[:, None]
    cols = jnp.arange(N_HEADS * LANES)[None, :]
    w["selq"] = jnp.where(valid & (cols == colq[:, None]), 1.0, 0.0).astype(BF16)
    w["selk"] = jnp.where(valid & (cols == colk[:, None]), -1.0, 0.0).astype(BF16)
    within = cols % LANES
    w["oneq"] = jnp.where((within >= HEAD_DIM + 3) & (within < HEAD_DIM + 6), 1.0, 0.0).astype(F32)
    w["onek"] = jnp.where((within >= HEAD_DIM) & (within < HEAD_DIM + 3), 1.0, 0.0).astype(F32)
    return w


def kernel(x_prompt, x_sample, cache_k, cache_v, cache_logf, page_table, state_ssm_re, state_ssm_im, state_conv,
           g_mix, w_in, b_f, lam_re, lam_im, log_dt, b_re, b_im, c_re, c_im, d_skip, w_glu_a, w_glu_b,
           w_attn_out, w_out, g_ffn, w_gate, w_up, w_conv, b_conv, w_down, g_final):
    depth = w_in.shape[0]
    assert depth == 1, "final norm is fused into the layer kernel; one layer supported"
    l = 0
    Bp, T, D = x_prompt.shape
    Bd, Td, _ = x_sample.shape
    n_pool = cache_k.shape[1]
    G, P = lam_re.shape[1], lam_re.shape[2]
    C = SSM_GROUP
    F = w_gate.shape[2]
    R = S5_BLOCK
    assert T % R == 0

    w = _pack_weights(l, g_mix, w_in, b_f, w_glu_a, w_glu_b, w_attn_out, w_out, g_ffn, w_gate, w_up, w_conv,
                      b_conv, w_down, g_final)
    gm = g_mix[l].reshape(1, D)
    sp = {"lam_re": lam_re[l], "lam_im": lam_im[l], "log_dt": log_dt[l], "b_re": b_re[l], "b_im": b_im[l],
          "c_re": c_re[l], "c_im": c_im[l], "d_skip": d_skip[l]}

    k_p, v_p, lf_p, u_p, sga_p, sgs_p, qa, ka, va = _in_proj(x_prompt, gm, w, prompt=True)
    attn_p = _attn_prompt(qa, ka, va)
    ug = u_p.reshape(Bp, T // R, R, G, C).transpose(0, 3, 1, 2, 4).reshape(Bp, G, T // R, R * C)
    yg, hfin = _s5_prompt(ug, _s5_block_weights(sp))
    s_p = yg.reshape(Bp, G, T // R, R, C).transpose(0, 2, 3, 1, 4).reshape(Bp, T, G * C)
    zeros_c = jnp.zeros((Bp, CONV_W - 1, F), F32)
    y_p, conv_p = _mix_ffn(x_prompt, s_p, attn_p, sga_p, sgs_p, zeros_c, w, shift=1, tm_pref=256)

    xs_tm = x_sample.transpose(1, 0, 2).reshape(1, Td * Bd, D)
    q_s, k_s, v_s, lf_s, u_s, sga_s, sgs_s = _in_proj(xs_tm, gm, w, prompt=False)
    to_bt = lambda a: a.reshape(Td, Bd, a.shape[-1]).transpose(1, 0, 2)
    q_b, k_b, v_b, lf_b = to_bt(q_s), to_bt(k_s), to_bt(v_s), to_bt(lf_s)
    padk = lambda a: jnp.pad(a, ((0, 0), (0, PAGE_SIZE - Td), (0, 0)))
    lf_new_t = jnp.pad(lf_b.transpose(0, 2, 1), ((0, 0), (0, 0), (0, PAGE_SIZE - Td)))
    attn_s = _attn_sample(page_table, q_b, padk(k_b), padk(v_b), lf_new_t,
                          cache_k[l].reshape(n_pool, PAGE_SIZE, D_ATTN), cache_v[l].reshape(n_pool, PAGE_SIZE, D_ATTN),
                          cache_logf[l].transpose(0, 2, 1))
    h0 = jnp.concatenate([state_ssm_re[l].reshape(Bd, G * P), state_ssm_im[l].reshape(Bd, G * P)], axis=1)
    s_s, ht = _s5_sample(u_s.reshape(Td * Bd, G * C), h0, sp, Td)
    attn_s_tm = attn_s.transpose(1, 0, 2).reshape(1, Td * Bd, D_ATTN).astype(BF16)
    prev_s = state_conv[l].transpose(1, 0, 2).reshape(1, (CONV_W - 1) * Bd, F)
    y_s, conv_s = _mix_ffn(xs_tm, s_s.reshape(1, Td * Bd, G * C), attn_s_tm, sga_s, sgs_s, prev_s, w,
                           shift=Bd, tm_pref=256)

    y_sample = y_s.reshape(Td, Bd, D).transpose(1, 0, 2)
    conv_sample = conv_s.reshape(CONV_W - 1, Bd, F).transpose(1, 0, 2)
    hd = lambda a, b, t: a.reshape(1, b, t, N_HEADS, HEAD_DIM)
    return (y_p, y_sample,
            hd(k_p, Bp, T), hd(v_p, Bp, T), lf_p.reshape(1, Bp, T, N_HEADS),
            hd(k_b, Bd, Td), hd(v_b, Bd, Td), lf_b.reshape(1, Bd, Td, N_HEADS),
            hfin[:, :, :P].reshape(1, Bp, G, P), hfin[:, :, P:].reshape(1, Bp, G, P),
            ht[:, :G * P].reshape(1, Bd, G, P), ht[:, G * P:].reshape(1, Bd, G, P),
            conv_p.reshape(1, Bp, CONV_W - 1, F), conv_sample.reshape(1, Bd, CONV_W - 1, F))
```

```python
import functools

import jax
import jax.numpy as jnp
from jax import lax
from jax.experimental import pallas as pl
from jax.experimental.pallas import tpu as pltpu

N_HEADS = 8
HEAD_DIM = 64
D_ATTN = N_HEADS * HEAD_DIM
SSM_GROUP = 16
STATE_P = 64
CONV_W = 3
PAGE_SIZE = 128
RMS_EPS = 1e-6
NEG_INF = -1e30
SCALE = HEAD_DIM ** -0.5
LOG2E = 1.4426950408889634

LANES = 128
S5_BLOCK = 16
VMEM_LIMIT = 56 * 1024 * 1024

F32 = jnp.float32
BF16 = jnp.bfloat16
HI = lax.Precision.HIGHEST


def _dot(a, b):
    return jnp.dot(a, b, preferred_element_type=F32)


def _dot_nt(a, b):
    return lax.dot_general(a, b, (((1,), (1,)), ((), ())), preferred_element_type=F32)


def _split3(x):
    hi = x.astype(BF16)
    r1 = x - hi.astype(F32)
    mid = r1.astype(BF16)
    lo = (r1 - mid.astype(F32)).astype(BF16)
    return hi, mid, lo


def _rmsnorm(x, g):
    return x * lax.rsqrt(jnp.mean(x * x, axis=-1, keepdims=True) + RMS_EPS) * g


def _pick_tile(n, pref):
    t = min(n, pref)
    while n % t:
        t //= 2
    return t


def _const_spec(shape):
    nd = len(shape)
    return pl.BlockSpec(shape, lambda *_: (0,) * nd, pipeline_mode=pl.Buffered(1))


def _params(sem):
    return pltpu.CompilerParams(dimension_semantics=sem, vmem_limit_bytes=VMEM_LIMIT)


def _in_proj_kernel(*refs, prompt, tm):
    if prompt:
        (x_ref, g_ref, wqkv_ref, wf_ref, bf_ref, wu_ref, wga_ref, wgs_ref, wqt_ref, wvt_ref, selk_ref,
         k_ref, v_ref, lf_ref, u_ref, sga_ref, sgs_ref, qat_ref, ka_ref, vat_ref, carry_ref) = refs
    else:
        (x_ref, g_ref, wqkv_ref, wf_ref, bf_ref, wu_ref, wga_ref, wgs_ref,
         q_ref, k_ref, v_ref, lf_ref, u_ref, sga_ref, sgs_ref) = refs

    hb = _rmsnorm(x_ref[...], g_ref[...]).astype(BF16)
    logf = jax.nn.log_sigmoid(_dot(hb, wf_ref[...]) + bf_ref[...])
    lf_ref[...] = logf[:, :N_HEADS]
    u = _dot(hb, wu_ref[...])
    for qd in range(u.shape[1] // LANES):
        u_ref[qd] = u[:, qd * LANES:(qd + 1) * LANES]
    sga_ref[...] = jax.nn.sigmoid(_dot(hb, wga_ref[...]))
    sgs_ref[...] = jax.nn.sigmoid(_dot(hb, wgs_ref[...]))
    if not prompt:
        qkv = _dot(hb, wqkv_ref[...])
        q_ref[...] = qkv[:, :D_ATTN]
        k_ref[...] = qkv[:, D_ATTN:2 * D_ATTN]
        v_ref[...] = qkv[:, 2 * D_ATTN:]
        return

    kv = _dot(hb, wqkv_ref[:, D_ATTN:])
    k = kv[:, :D_ATTN]
    k_ref[...] = k
    v_ref[...] = kv[:, D_ATTN:]

    @pl.when(pl.program_id(1) == 0)
    def _():
        carry_ref[...] = jnp.zeros_like(carry_ref)

    row = lax.broadcasted_iota(jnp.int32, (tm, tm), 0)
    col = lax.broadcasted_iota(jnp.int32, (tm, tm), 1)
    tri = jnp.where(row >= col, 1.0, 0.0).astype(BF16)
    l_hi, l_mid, l_lo = _split3(logf)
    c = _dot(tri, l_hi) + _dot(tri, l_mid) + _dot(tri, l_lo) + carry_ref[...]
    carry_ref[...] = c[tm - 1:tm, :]

    e = jnp.concatenate(_split3(c * LOG2E), axis=1)
    ek = _dot(e, selk_ref[...])
    lane = lax.broadcasted_iota(jnp.int32, (tm, LANES), 1)
    low = lane < HEAD_DIM
    for h in range(N_HEADS):
        t = h // 2
        kt = k[:, t * LANES:(t + 1) * LANES]
        if h % 2:
            kt = pltpu.roll(kt, HEAD_DIM, 1)
        ka_ref[h] = jnp.where(low, kt, ek[:, h * LANES:(h + 1) * LANES]).astype(BF16)
    qt = _dot_nt(wqt_ref[...], hb) * (SCALE * LOG2E)
    vt = _dot_nt(wvt_ref[...], hb)
    sub = lax.broadcasted_iota(jnp.int32, (HEAD_DIM, tm), 0)
    q_ones = jnp.where(sub < 3, 1.0, 0.0)
    v_ones = jnp.where(sub == 0, 1.0, 0.0)
    for h in range(N_HEADS):
        rows = slice(h * HEAD_DIM, (h + 1) * HEAD_DIM)
        qat_ref[h] = jnp.concatenate([qt[rows], q_ones], axis=0).astype(BF16)
        vat_ref[h] = jnp.concatenate([vt[rows], v_ones], axis=0).astype(BF16)


def _in_proj(x, g_mix, w, *, prompt, attn_blk=None):
    B, T, D = x.shape
    tm = _pick_tile(T, 256)
    nt = T // tm
    d_gate = w["wga"].shape[1]
    d_ssm = w["wu"].shape[1]
    row_spec = lambda n: pl.BlockSpec((None, tm, n), lambda b, t: (b, t, 0))
    ins = [x, g_mix, w["wqkv"], w["wf"], w["bf"], w["wu"], w["wga"], w["wgs"]]
    in_specs = [row_spec(D)] + [_const_spec(a.shape) for a in ins[1:]]
    outs = [jax.ShapeDtypeStruct((B, T, D_ATTN), F32),
            jax.ShapeDtypeStruct((B, T, D_ATTN), F32),
            jax.ShapeDtypeStruct((B, T, N_HEADS), F32),
            jax.ShapeDtypeStruct((B, d_ssm // LANES, T, LANES), F32),
            jax.ShapeDtypeStruct((B, T, d_gate), F32),
            jax.ShapeDtypeStruct((B, T, d_gate), F32)]
    out_specs = [row_spec(D_ATTN), row_spec(D_ATTN), row_spec(N_HEADS),
                 pl.BlockSpec((None, d_ssm // LANES, tm, LANES), lambda b, t: (b, 0, t, 0)),
                 row_spec(d_gate), row_spec(d_gate)]
    scratch = []
    if prompt:
        extra = [w["wqt"], w["wvt"], w["selk"]]
        ins += extra
        in_specs += [_const_spec(a.shape) for a in extra]
        per = attn_blk // tm
        outs += [jax.ShapeDtypeStruct((B, N_HEADS, LANES, T), BF16),
                 jax.ShapeDtypeStruct((B, N_HEADS, T, LANES), BF16),
                 jax.ShapeDtypeStruct((B, N_HEADS, T // attn_blk, LANES, attn_blk), BF16)]
        out_specs += [pl.BlockSpec((None, N_HEADS, LANES, tm), lambda b, t: (b, 0, 0, t)),
                      pl.BlockSpec((None, N_HEADS, tm, LANES), lambda b, t: (b, 0, t, 0)),
                      pl.BlockSpec((None, N_HEADS, None, LANES, tm), lambda b, t: (b, 0, t // per, 0, t % per))]
        scratch = [pltpu.VMEM((1, LANES), F32)]
    else:
        outs = [jax.ShapeDtypeStruct((B, T, D_ATTN), F32)] + outs
        out_specs = [row_spec(D_ATTN)] + out_specs
    return pl.pallas_call(
        functools.partial(_in_proj_kernel, prompt=prompt, tm=tm),
        grid=(B, nt), in_specs=in_specs, out_specs=out_specs, out_shape=outs, scratch_shapes=scratch,
        compiler_params=_params(("arbitrary", "arbitrary")),
        name="in_proj_prompt" if prompt else "in_proj_sample",
    )(*ins)


def _attn_prompt_kernel(qt_ref, k_ref, vt_ref, o_ref, acc0_ref, acc1_ref, *, blk):
    qi = pl.program_id(2)
    accs = (acc0_ref, acc1_ref)
    key = lax.broadcasted_iota(jnp.int32, (blk, blk), 0)
    qry = lax.broadcasted_iota(jnp.int32, (blk, blk), 1)
    for a in accs:
        a[...] = jnp.zeros_like(a)

    def update(j, ms, masked):
        start = pl.multiple_of(j * blk, blk)
        new_ms = []
        for hh in range(2):
            st = _dot(k_ref[hh, pl.ds(start, blk), :], qt_ref[hh])
            if masked:
                st = jnp.where(key <= qry, st, NEG_INF)
            m_new = jnp.maximum(ms[hh], jnp.max(st, axis=0, keepdims=True))
            p = jnp.exp2(st - m_new).astype(BF16)
            accs[hh][...] = jnp.exp2(ms[hh] - m_new) * accs[hh][...] + _dot(vt_ref[hh, j], p)
            new_ms.append(m_new)
        return tuple(new_ms)

    m0 = jnp.full((1, blk), NEG_INF, F32)
    ms = lax.fori_loop(0, qi, lambda j, ms: update(j, ms, False), (m0, m0))
    update(qi, ms, True)
    outs = []
    for a in accs:
        acc = a[...]
        outs.append(acc[:HEAD_DIM] / acc[HEAD_DIM:HEAD_DIM + 1])
    o_ref[...] = jnp.concatenate(outs, axis=0).T.astype(o_ref.dtype)


def _attn_prompt(qat, ka, vat):
    B, H, T, _ = ka.shape
    nk, blk = vat.shape[2], vat.shape[4]
    return pl.pallas_call(
        functools.partial(_attn_prompt_kernel, blk=blk),
        grid=(B, H // 2, nk),
        in_specs=[pl.BlockSpec((None, 2, LANES, blk), lambda b, hp, qi: (b, hp, 0, qi)),
                  pl.BlockSpec((None, 2, T, LANES), lambda b, hp, qi: (b, hp, 0, 0)),
                  pl.BlockSpec((None, 2, nk, LANES, blk), lambda b, hp, qi: (b, hp, 0, 0, 0))],
        out_specs=pl.BlockSpec((None, blk, LANES), lambda b, hp, qi: (b, qi, hp)),
        out_shape=jax.ShapeDtypeStruct((B, T, D_ATTN), BF16),
        scratch_shapes=[pltpu.VMEM((LANES, blk), F32), pltpu.VMEM((LANES, blk), F32)],
        compiler_params=_params(("arbitrary", "arbitrary", "arbitrary")),
        name="attn_prompt",
    )(qat, ka, vat)


def _attn_sample_kernel(pt_ref, q_ref, kn_ref, vn_ref, lfn_ref, *rest, pg, tq):
    k_refs = rest[:pg]
    v_refs = rest[pg:2 * pg]
    lf_refs = rest[2 * pg:3 * pg]
    o_ref, acc_ref, m_ref, l_ref, coff_ref = rest[3 * pg:]
    del pt_ref
    j = pl.program_id(1)
    hq = N_HEADS * tq
    nk = PAGE_SIZE * N_HEADS
    per = LANES // N_HEADS
    nr = nk // LANES

    @pl.when(j == 0)
    def _():
        acc_ref[...] = jnp.zeros_like(acc_ref)
        l_ref[...] = jnp.zeros_like(l_ref)
        m_ref[...] = jnp.full_like(m_ref, NEG_INF)
        coff_ref[...] = jnp.zeros_like(coff_ref)

    q = (q_ref[...] * LOG2E).astype(BF16)
    ri = lax.broadcasted_iota(jnp.int32, (LANES, LANES), 0)
    ci = lax.broadcasted_iota(jnp.int32, (LANES, LANES), 1)
    same_head = (ri % N_HEADS) == (ci % N_HEADS)
    t_within = jnp.where(same_head & (ri // N_HEADS <= ci // N_HEADS), 1.0, 0.0)
    t_rowsum = jnp.where(same_head, 1.0, 0.0)
    t12 = jnp.concatenate([t_within, t_rowsum], axis=1).astype(BF16)
    sub = lax.broadcasted_iota(jnp.int32, (nr, LANES), 0)
    head_bias = jnp.where(lax.broadcasted_iota(jnp.int32, (hq, LANES), 0) // tq
                          == lax.broadcasted_iota(jnp.int32, (hq, LANES), 1) % N_HEADS, 0.0, NEG_INF)

    def cum_logf(lf_list, coff):
        lf = jnp.concatenate(lf_list, axis=0) if len(lf_list) > 1 else lf_list[0]
        a, b, c3 = _split3(lf)
        w = _dot(a, t12) + _dot(b, t12) + _dot(c3, t12)
        pages = []
        for i in range(len(lf_list)):
            within = w[i * nr:(i + 1) * nr, :LANES]
            rowsum = w[i * nr:(i + 1) * nr, LANES:]
            inc = rowsum
            for sh in (1, 2, 4):
                inc = inc + jnp.where(sub >= sh, pltpu.roll(inc, sh, 0), 0.0)
            pages.append((within + (inc - rowsum) + coff) * LOG2E)
            coff = coff + inc[nr - 1:nr, :]
        return pages, coff

    def page_scores(k_page, cpage, allowed=None):
        s = _dot_nt(q, k_page.reshape(nk, HEAD_DIM).astype(BF16))
        tiles = []
        for r in range(nr):
            t = s[:, r * LANES:(r + 1) * LANES] - cpage[r:r + 1, :] + head_bias
            if allowed is not None:
                t = jnp.where(allowed[r], t, NEG_INF)
            tiles.append(t)
        return tiles

    def online_update(tiles, v_list):
        s = jnp.concatenate(tiles, axis=1)
        m_prev = m_ref[...]
        m_new = jnp.maximum(m_prev, jnp.max(s, axis=1, keepdims=True))
        p = jnp.exp2(s - m_new)
        alpha = jnp.exp2(m_prev - m_new)
        pv = None
        for i, v_page in enumerate(v_list):
            d = _dot(p[:, i * nk:(i + 1) * nk].astype(BF16), v_page.reshape(nk, HEAD_DIM).astype(BF16))
            pv = d if pv is None else pv + d
        acc_ref[...] = alpha * acc_ref[...] + pv
        l_ref[...] = alpha * l_ref[...] + jnp.sum(p, axis=1, keepdims=True)
        m_ref[...] = m_new

    cpages, coff = cum_logf([lf_refs[i][...] for i in range(pg)], coff_ref[...])
    coff_ref[...] = coff
    tiles = []
    for i in range(pg):
        tiles += page_scores(k_refs[i][...], cpages[i])
    online_update(tiles, [v_refs[i][...] for i in range(pg)])

    @pl.when(j == pl.num_programs(1) - 1)
    def _():
        cnew, _unused = cum_logf([lfn_ref[...]], coff_ref[...])
        qtok = lax.broadcasted_iota(jnp.int32, (hq, LANES), 0) % tq
        lane_key = lax.broadcasted_iota(jnp.int32, (hq, LANES), 1) // N_HEADS
        allowed = [(r * per + lane_key) <= qtok for r in range(nr)]
        online_update(page_scores(kn_ref[...], cnew[0], allowed), [vn_ref[...]])
        o_ref[...] = acc_ref[...] / l_ref[...]


def _attn_sample(page_table, q, k_new, v_new, lf_new, cache_k, cache_v, cache_lf, layer):
    Bd, hq, _ = q.shape
    tq = hq // N_HEADS
    n_pages = page_table.shape[1]
    pg = _pick_tile(n_pages, 8)
    ng = n_pages // pg
    pt_flat = page_table.reshape(-1)
    nr = PAGE_SIZE * N_HEADS // LANES

    def kv_spec(i):
        return pl.BlockSpec((None, None, PAGE_SIZE, N_HEADS, HEAD_DIM),
                            lambda b, j, pt: (layer, pt[b * n_pages + j * pg + i], 0, 0, 0))

    def lf_spec(i):
        return pl.BlockSpec((None, None, nr, LANES), lambda b, j, pt: (layer, pt[b * n_pages + j * pg + i], 0, 0))

    in_specs = ([pl.BlockSpec((None, hq, HEAD_DIM), lambda b, j, pt: (b, 0, 0)),
                 pl.BlockSpec((None, PAGE_SIZE, N_HEADS, HEAD_DIM), lambda b, j, pt: (b, 0, 0, 0)),
                 pl.BlockSpec((None, PAGE_SIZE, N_HEADS, HEAD_DIM), lambda b, j, pt: (b, 0, 0, 0)),
                 pl.BlockSpec((None, nr, LANES), lambda b, j, pt: (b, 0, 0))]
                + [kv_spec(i) for i in range(pg)] + [kv_spec(i) for i in range(pg)]
                + [lf_spec(i) for i in range(pg)])
    grid_spec = pltpu.PrefetchScalarGridSpec(
        num_scalar_prefetch=1, grid=(Bd, ng), in_specs=in_specs,
        out_specs=pl.BlockSpec((None, hq, HEAD_DIM), lambda b, j, pt: (b, 0, 0)),
        scratch_shapes=[pltpu.VMEM((hq, HEAD_DIM), F32), pltpu.VMEM((hq, 1), F32), pltpu.VMEM((hq, 1), F32),
                        pltpu.VMEM((1, LANES), F32)])
    return pl.pallas_call(
        functools.partial(_attn_sample_kernel, pg=pg, tq=tq),
        grid_spec=grid_spec,
        out_shape=jax.ShapeDtypeStruct((Bd, hq, HEAD_DIM), F32),
        compiler_params=_params(("arbitrary", "arbitrary")),
        name="attn_sample",
    )(pt_flat, q, k_new, v_new, lf_new, *([cache_k] * pg), *([cache_v] * pg), *([cache_lf] * pg))


def _s5_discretize(p):
    lr, li = p["lam_re"], p["lam_im"]
    dt = jnp.exp(p["log_dt"])[:, None]
    mag = jnp.exp(lr * dt)
    ang = li * dt
    ab_re = mag * jnp.cos(ang)
    ab_im = mag * jnp.sin(ang)
    den = lr * lr + li * li
    n_re = ab_re - 1.0
    n_im = ab_im
    f_re = (n_re * lr + n_im * li) / den
    f_im = (n_im * lr - n_re * li) / den
    bb_re = f_re[..., None] * p["b_re"] - f_im[..., None] * p["b_im"]
    bb_im = f_re[..., None] * p["b_im"] + f_im[..., None] * p["b_re"]
    return ab_re, ab_im, bb_re, bb_im


def _s5_block_weights(p):
    R = S5_BLOCK
    ab_re, ab_im, bb_re, bb_im = _s5_discretize(p)
    G, P, C = bb_re.shape
    pw_re, pw_im = [jnp.ones_like(ab_re)], [jnp.zeros_like(ab_im)]
    for _ in range(R):
        pr, pi = pw_re[-1], pw_im[-1]
        pw_re.append(pr * ab_re - pi * ab_im)
        pw_im.append(pr * ab_im + pi * ab_re)
    pw_re = jnp.stack(pw_re)
    pw_im = jnp.stack(pw_im)
    c_re, c_im = p["c_re"], p["c_im"]
    cp_re = c_re[None] * pw_re[:, :, None, :] - c_im[None] * pw_im[:, :, None, :]
    cp_im = c_re[None] * pw_im[:, :, None, :] + c_im[None] * pw_re[:, :, None, :]
    m = (jnp.einsum('jgcp,gpd->jgdc', cp_re[:R], bb_re, precision=HI)
         - jnp.einsum('jgcp,gpd->jgdc', cp_im[:R], bb_im, precision=HI))
    lag = jnp.arange(R)[None, :] - jnp.arange(R)[:, None]
    kt = jnp.where((lag >= 0)[:, :, None, None, None], m[jnp.clip(lag, 0, R - 1)], 0.0)
    kt = kt.transpose(2, 0, 3, 1, 4).reshape(G, R * C, R * C)
    pb_re = pw_re[R - 1::-1][:R]
    pb_im = pw_im[R - 1::-1][:R]
    x_re = pb_re[..., None] * bb_re[None] - pb_im[..., None] * bb_im[None]
    x_im = pb_re[..., None] * bb_im[None] + pb_im[..., None] * bb_re[None]
    x_re = x_re.transpose(1, 0, 3, 2).reshape(G, R * C, P)
    x_im = x_im.transpose(1, 0, 3, 2).reshape(G, R * C, P)
    wx = jnp.concatenate([x_re, x_im, x_im, x_re], axis=-1)
    wc_re = cp_re[1:].transpose(1, 3, 0, 2).reshape(G, P, R * C)
    wc_im = cp_im[1:].transpose(1, 3, 0, 2).reshape(G, P, R * C)
    wc = jnp.concatenate([wc_re, -wc_im], axis=1)
    a_blk = jnp.concatenate([pw_re[R], pw_re[R]], axis=-1)
    b_blk = jnp.concatenate([-pw_im[R], pw_im[R]], axis=-1)
    d = jnp.tile(p["d_skip"].reshape(G, 1, C), (1, 1, R))
    return kt.astype(BF16), wx.astype(BF16), wc.astype(BF16), a_blk, b_blk, d


def _s5_prompt_kernel(u_ref, kt_ref, wx_ref, wc_ref, a_ref, b_ref, d_ref, y_ref, hfin_ref,
                      ug_ref, xa_ref, xb_ref, hp_ref, h_ref, hsw_ref, *, tb, ng):
    R, C = S5_BLOCK, SSM_GROUP
    gpt = LANES // C
    sp = 2 * STATE_P

    @pl.when(pl.program_id(1) == 0)
    def _():
        h_ref[...] = jnp.zeros_like(h_ref)
        hsw_ref[...] = jnp.zeros_like(hsw_ref)

    lane_grp = lax.broadcasted_iota(jnp.int32, (tb, LANES), 1) // C
    for qd in range(ng // gpt):
        for half in range(R // gpt):
            for gl in range(gpt):
                acc = None
                for r8 in range(gpt):
                    a_r = u_ref[qd, pl.ds(half * gpt + r8, tb, stride=R), :]
                    sh = ((r8 - gl) * C) % LANES
                    rolled = pltpu.roll(a_r, sh, 1) if sh else a_r
                    acc = rolled if acc is None else jnp.where(lane_grp == r8, rolled, acc)
                ug_ref[qd * gpt + gl, :, half * LANES:(half + 1) * LANES] = acc

    def xbody(g, carry):
        x = _dot(ug_ref[g].astype(BF16), wx_ref[g])
        rows = pl.ds(pl.multiple_of(g * tb, tb), tb)
        xa_ref[rows, :] = x[:, :sp]
        xb_ref[rows, :] = x[:, sp:]
        return carry

    lax.fori_loop(0, ng, xbody, 0)

    a = a_ref[...]
    b = b_ref[...]

    def sbody(k, carry):
        h, hsw = carry
        rows = pl.ds(k, ng, stride=tb)
        hp_ref[rows, :] = h
        return a * h + b * hsw + xa_ref[rows, :], a * hsw - b * h + xb_ref[rows, :]

    h, hsw = lax.fori_loop(0, tb, sbody, (h_ref[...], hsw_ref[...]))
    h_ref[...] = h
    hsw_ref[...] = hsw
    hfin_ref[...] = h

    def ybody(g, carry):
        u = ug_ref[g]
        hp = hp_ref[pl.ds(pl.multiple_of(g * tb, tb), tb), :].astype(BF16)
        ug_ref[g] = _dot(u.astype(BF16), kt_ref[g]) + _dot(hp, wc_ref[g]) + u * d_ref[g]
        return carry

    lax.fori_loop(0, ng, ybody, 0)

    for qd in range(ng // gpt):
        for r in range(R):
            half, r8 = divmod(r, gpt)
            acc = None
            for gl in range(gpt):
                y_t = ug_ref[qd * gpt + gl, :, half * LANES:(half + 1) * LANES]
                sh = ((gl - r8) * C) % LANES
                rolled = pltpu.roll(y_t, sh, 1) if sh else y_t
                acc = rolled if acc is None else jnp.where(lane_grp == gl, rolled, acc)
            y_ref[qd, pl.ds(r, tb, stride=R), :] = acc


def _s5_prompt(u, weights):
    kt, wx, wc, a_blk, b_blk, d = weights
    B, nq, T, _ = u.shape
    G = kt.shape[0]
    R = S5_BLOCK
    tb = _pick_tile(T // R, 128)
    sp = 2 * STATE_P
    blk_spec = pl.BlockSpec((None, nq, tb * R, LANES), lambda b, t: (b, 0, t, 0))
    return pl.pallas_call(
        functools.partial(_s5_prompt_kernel, tb=tb, ng=G),
        grid=(B, T // (tb * R)),
        in_specs=[blk_spec] + [_const_spec(x.shape) for x in (kt, wx, wc, a_blk, b_blk, d)],
        out_specs=[blk_spec, pl.BlockSpec((None, G, sp), lambda b, t: (b, 0, 0))],
        out_shape=[jax.ShapeDtypeStruct(u.shape, F32), jax.ShapeDtypeStruct((B, G, sp), F32)],
        scratch_shapes=[pltpu.VMEM((G, tb, R * SSM_GROUP), F32),
                        pltpu.VMEM((G * tb, sp), F32), pltpu.VMEM((G * tb, sp), F32), pltpu.VMEM((G * tb, sp), F32),
                        pltpu.VMEM((G, sp), F32), pltpu.VMEM((G, sp), F32)],
        compiler_params=_params(("arbitrary", "arbitrary")),
        name="s5_prompt",
    )(u, kt, wx, wc, a_blk, b_blk, d)


def _s5_sample_kernel(u_ref, h0_ref, bbd_ref, cmat_ref, are_ref, aim_ref, d_ref, y_ref, ht_ref, hs_ref, *, nt, nb):
    ns = are_ref.shape[1]
    u = u_ref[...]
    ub = u.astype(BF16)
    cw = 256
    for c0 in range(0, ns, cw):
        x_re = _dot(ub, bbd_ref[:, c0:c0 + cw])
        x_im = _dot(ub, bbd_ref[:, ns + c0:ns + c0 + cw])
        are = are_ref[:, c0:c0 + cw]
        aim = aim_ref[:, c0:c0 + cw]
        hr = h0_ref[:, c0:c0 + cw]
        hi = h0_ref[:, ns + c0:ns + c0 + cw]
        for t in range(nt):
            rows = slice(t * nb, (t + 1) * nb)
            hr, hi = are * hr - aim * hi + x_re[rows], are * hi + aim * hr + x_im[rows]
            hs_ref[rows, c0:c0 + cw] = hr.astype(BF16)
            hs_ref[rows, ns + c0:ns + c0 + cw] = hi.astype(BF16)
        ht_ref[:, c0:c0 + cw] = hr
        ht_ref[:, ns + c0:ns + c0 + cw] = hi
    y_ref[...] = _dot(hs_ref[...], cmat_ref[...]) + u * d_ref[...]


def _s5_sample(u_tm, h0, p, nt):
    ab_re, ab_im, bb_re, bb_im = _s5_discretize(p)
    G, P, C = bb_re.shape
    eye = jnp.eye(G, dtype=F32)
    bd = lambda w: jnp.einsum('gpc,gk->gckp', w, eye).reshape(G * C, G * P)
    bbd = jnp.concatenate([bd(bb_re), bd(bb_im)], axis=1).astype(BF16)
    cd = lambda w: jnp.einsum('gcp,gk->gpkc', w, eye).reshape(G * P, G * C)
    cmat = jnp.concatenate([cd(p["c_re"]), -cd(p["c_im"])], axis=0).astype(BF16)
    nb = h0.shape[0]
    ins = [u_tm, h0, bbd, cmat, ab_re.reshape(1, G * P), ab_im.reshape(1, G * P), p["d_skip"].reshape(1, G * C)]
    return pl.pallas_call(
        functools.partial(_s5_sample_kernel, nt=nt, nb=nb),
        grid=(1,),
        in_specs=[_const_spec(x.shape) for x in ins],
        out_specs=[pl.BlockSpec(u_tm.shape, lambda i: (0, 0)), pl.BlockSpec(h0.shape, lambda i: (0, 0))],
        out_shape=[jax.ShapeDtypeStruct(u_tm.shape, F32), jax.ShapeDtypeStruct(h0.shape, F32)],
        scratch_shapes=[pltpu.VMEM((nt * nb, 2 * G * P), BF16)],
        compiler_params=_params(("arbitrary",)),
        name="s5_sample",
    )(*ins)


def _mix_ffn_kernel(x_ref, s_ref, attn_ref, sga_ref, sgs_ref, prev_ref,
                    wa_ref, wb_ref, wao_ref, wout_ref, gffn_ref, wg_ref, wup_ref, wconv_ref, bconv_ref, wd_ref,
                    gfin_ref, y_ref, conv_ref, gate_ref, *, tm, shift, pad):
    hist = (CONV_W - 1) * shift

    @pl.when(pl.program_id(1) == 0)
    def _():
        gate_ref[pad - hist:pad, :] = prev_ref[...]

    s = jnp.concatenate([s_ref[qd] for qd in range(s_ref.shape[0])], axis=1)
    zb = jax.nn.gelu(s).astype(BF16)
    ssm_out = _dot(zb, wa_ref[...]) * jax.nn.sigmoid(_dot(zb, wb_ref[...]))
    attn_out = _dot(attn_ref[...], wao_ref[...])
    merged = sgs_ref[...] * ssm_out + sga_ref[...] * attn_out
    x1 = x_ref[...] + _dot(merged.astype(BF16), wout_ref[...])
    h2 = _rmsnorm(x1, gffn_ref[...]).astype(BF16)
    gate_ref[pad:pad + tm, :] = _dot(h2, wg_ref[...])
    gc = gate_ref[pad - hist:pad - hist + tm, :] * wconv_ref[0:1, :]
    for i in range(1, CONV_W):
        gc = gc + gate_ref[pad - hist + i * shift:pad - hist + i * shift + tm, :] * wconv_ref[i:i + 1, :]
    gc = bconv_ref[...] + gc
    act = (jax.nn.gelu(gc) * _dot(h2, wup_ref[...])).astype(BF16)
    x2 = x1 + _dot(act, wd_ref[...])
    y_ref[...] = _rmsnorm(x2, gfin_ref[...])
    tail = gate_ref[pad + tm - hist:pad + tm, :]
    conv_ref[...] = tail
    gate_ref[pad - hist:pad, :] = tail


def _mix_ffn(x, s, attn, sga, sgs, prev, w, *, shift, tm_pref):
    B, T, D = x.shape
    tm = _pick_tile(T, tm_pref)
    hist = (CONV_W - 1) * shift
    pad = -(-hist // 8) * 8
    F = w["wg"].shape[1]
    row_spec = lambda n: pl.BlockSpec((None, tm, n), lambda b, t: (b, t, 0))
    seq_spec = pl.BlockSpec((None, hist, F), lambda b, t: (b, 0, 0))
    consts = [w["wa"], w["wb"], w["wao"], w["wout"], w["gffn"], w["wg"], w["wup"], w["wconv"], w["bconv"], w["wd"],
              w["gfin"]]
    return pl.pallas_call(
        functools.partial(_mix_ffn_kernel, tm=tm, shift=shift, pad=pad),
        grid=(B, T // tm),
        in_specs=[row_spec(D), pl.BlockSpec((None, s.shape[1], tm, LANES), lambda b, t: (b, 0, t, 0)),
                  row_spec(attn.shape[2]), row_spec(D), row_spec(D), seq_spec]
        + [_const_spec(a.shape) for a in consts],
        out_specs=[row_spec(D), seq_spec],
        out_shape=[jax.ShapeDtypeStruct((B, T, D), F32), jax.ShapeDtypeStruct((B, hist, F), F32)],
        scratch_shapes=[pltpu.VMEM((pad + tm, F), F32)],
        compiler_params=_params(("arbitrary", "arbitrary")),
        name="mix_ffn_s%d" % shift,
    )(x, s, attn, sga, sgs, prev, *consts)


def _pack_weights(l, g_mix, w_in, b_f, w_glu_a, w_glu_b, w_attn_out, w_out, g_ffn, w_gate, w_up, w_conv, b_conv,
                  w_down, g_final):
    D = w_in.shape[1]
    d_ssm = w_glu_a.shape[1]
    o = [0, D_ATTN, 2 * D_ATTN, 3 * D_ATTN, 3 * D_ATTN + N_HEADS, 3 * D_ATTN + N_HEADS + d_ssm]
    wi = w_in[l]
    wq = wi[:, o[0]:o[1]] * SCALE
    w = {
        "wqkv": jnp.concatenate([wq, wi[:, o[1]:o[3]]], axis=1).astype(BF16),
        "wf": jnp.pad(wi[:, o[3]:o[4]], ((0, 0), (0, LANES - N_HEADS))).astype(BF16),
        "bf": jnp.pad(b_f[l], (0, LANES - N_HEADS)).reshape(1, LANES),
        "wu": wi[:, o[4]:o[5]].astype(BF16),
        "wga": wi[:, o[5]:o[5] + D].astype(BF16),
        "wgs": wi[:, o[5] + D:o[5] + 2 * D].astype(BF16),
        "wa": w_glu_a[l].astype(BF16), "wb": w_glu_b[l].astype(BF16), "wao": w_attn_out[l].astype(BF16),
        "wout": w_out[l].astype(BF16), "gffn": g_ffn[l].reshape(1, D), "wg": w_gate[l].astype(BF16),
        "wup": w_up[l].astype(BF16), "wconv": w_conv[l], "bconv": b_conv[l].reshape(1, -1),
        "wd": w_down[l].astype(BF16), "gfin": g_final.reshape(1, D),
    }
    w["wqt"] = wi[:, o[0]:o[1]].T.astype(BF16)
    w["wvt"] = wi[:, o[2]:o[3]].T.astype(BF16)
    e = jnp.arange(3 * LANES)
    part, head = e // LANES, e % LANES
    colk = head * LANES + HEAD_DIM + part
    cols = jnp.arange(N_HEADS * LANES)[None, :]
    w["selk"] = jnp.where((head < N_HEADS)[:, None] & (cols == colk[:, None]), -1.0, 0.0).astype(BF16)
    return w


def kernel(x_prompt, x_sample, cache_k, cache_v, cache_logf, page_table, state_ssm_re, state_ssm_im, state_conv,
           g_mix, w_in, b_f, lam_re, lam_im, log_dt, b_re, b_im, c_re, c_im, d_skip, w_glu_a, w_glu_b,
           w_attn_out, w_out, g_ffn, w_gate, w_up, w_conv, b_conv, w_down, g_final):
    depth = w_in.shape[0]
    assert depth == 1, "final norm is fused into the layer kernel; one layer supported"
    l = 0
    Bp, T, D = x_prompt.shape
    Bd, Td, _ = x_sample.shape
    n_pool = cache_k.shape[1]
    G, P = lam_re.shape[1], lam_re.shape[2]
    C = SSM_GROUP
    F = w_gate.shape[2]
    R = S5_BLOCK
    assert T % R == 0

    w = _pack_weights(l, g_mix, w_in, b_f, w_glu_a, w_glu_b, w_attn_out, w_out, g_ffn, w_gate, w_up, w_conv,
                      b_conv, w_down, g_final)
    gm = g_mix[l].reshape(1, D)
    sp = {"lam_re": lam_re[l], "lam_im": lam_im[l], "log_dt": log_dt[l], "b_re": b_re[l], "b_im": b_im[l],
          "c_re": c_re[l], "c_im": c_im[l], "d_skip": d_skip[l]}

    k_p, v_p, lf_p, u_p, sga_p, sgs_p, qat, ka, vat = _in_proj(x_prompt, gm, w, prompt=True,
                                                                 attn_blk=_pick_tile(T, 512))
    attn_p = _attn_prompt(qat, ka, vat)
    s_p, hfin = _s5_prompt(u_p, _s5_block_weights(sp))
    zeros_c = jnp.zeros((Bp, CONV_W - 1, F), F32)
    y_p, conv_p = _mix_ffn(x_prompt, s_p, attn_p, sga_p, sgs_p, zeros_c, w, shift=1, tm_pref=256)

    xs_tm = x_sample.transpose(1, 0, 2).reshape(1, Td * Bd, D)
    q_s, k_s, v_s, lf_s, u_s, sga_s, sgs_s = _in_proj(xs_tm, gm, w, prompt=False)
    to_bt = lambda a: a.reshape(Td, Bd, a.shape[-1]).transpose(1, 0, 2)
    q_b, k_b, v_b, lf_b = to_bt(q_s), to_bt(k_s), to_bt(v_s), to_bt(lf_s)
    heads = lambda a: a.reshape(Bd, Td, N_HEADS, HEAD_DIM)
    padk = lambda a: jnp.pad(heads(a), ((0, 0), (0, PAGE_SIZE - Td), (0, 0), (0, 0)))
    q_hq = heads(q_b).transpose(0, 2, 1, 3).reshape(Bd, N_HEADS * Td, HEAD_DIM)
    nr = PAGE_SIZE * N_HEADS // LANES
    lf_new = jnp.pad(lf_b, ((0, 0), (0, PAGE_SIZE - Td), (0, 0))).reshape(Bd, nr, LANES)
    attn_hq = _attn_sample(page_table, q_hq, padk(k_b), padk(v_b), lf_new, cache_k, cache_v,
                           cache_logf.reshape(depth, n_pool, nr, LANES), l)
    attn_s = attn_hq.reshape(Bd, N_HEADS, Td, HEAD_DIM).transpose(0, 2, 1, 3).reshape(Bd, Td, D_ATTN)
    h0 = jnp.concatenate([state_ssm_re[l].reshape(Bd, G * P), state_ssm_im[l].reshape(Bd, G * P)], axis=1)
    tiles = lambda a: a.reshape(a.shape[0], -1, LANES).transpose(1, 0, 2)
    s_s, ht = _s5_sample(u_s[0].transpose(1, 0, 2).reshape(Td * Bd, G * C), h0, sp, Td)
    attn_s_tm = attn_s.transpose(1, 0, 2).reshape(1, Td * Bd, D_ATTN).astype(BF16)
    prev_s = state_conv[l].transpose(1, 0, 2).reshape(1, (CONV_W - 1) * Bd, F)
    y_s, conv_s = _mix_ffn(xs_tm, tiles(s_s)[None], attn_s_tm, sga_s, sgs_s, prev_s, w,
                           shift=Bd, tm_pref=256)

    y_sample = y_s.reshape(Td, Bd, D).transpose(1, 0, 2)
    conv_sample = conv_s.reshape(CONV_W - 1, Bd, F).transpose(1, 0, 2)
    hd = lambda a, b, t: a.reshape(1, b, t, N_HEADS, HEAD_DIM)
    return (y_p, y_sample,
            hd(k_p, Bp, T), hd(v_p, Bp, T), lf_p.reshape(1, Bp, T, N_HEADS),
            hd(k_b, Bd, Td), hd(v_b, Bd, Td), lf_b.reshape(1, Bd, Td, N_HEADS),
            hfin[:, :, :P].reshape(1, Bp, G, P), hfin[:, :, P:].reshape(1, Bp, G, P),
            ht[:, :G * P].reshape(1, Bd, G, P), ht[:, G * P:].reshape(1, Bd, G, P),
            conv_p.reshape(1, Bp, CONV_W - 1, F), conv_sample.reshape(1, Bd, CONV_W - 1, F))
```

```python
import functools

import jax
import jax.numpy as jnp
from jax import lax
from jax.experimental import pallas as pl
from jax.experimental.pallas import tpu as pltpu

N_HEADS = 8
HEAD_DIM = 64
D_ATTN = N_HEADS * HEAD_DIM
SSM_GROUP = 16
STATE_P = 64
CONV_W = 3
PAGE_SIZE = 128
RMS_EPS = 1e-6
NEG_INF = -1e30
SCALE = HEAD_DIM ** -0.5
LOG2E = 1.4426950408889634

LANES = 128
S5_BLOCK = 16
VMEM_LIMIT = 56 * 1024 * 1024

F32 = jnp.float32
BF16 = jnp.bfloat16
HI = lax.Precision.HIGHEST


def _dot(a, b):
    return jnp.dot(a, b, preferred_element_type=F32)


def _dot_nt(a, b):
    return lax.dot_general(a, b, (((1,), (1,)), ((), ())), preferred_element_type=F32)


def _split3(x):
    hi = x.astype(BF16)
    r1 = x - hi.astype(F32)
    mid = r1.astype(BF16)
    lo = (r1 - mid.astype(F32)).astype(BF16)
    return hi, mid, lo


def _rmsnorm(x, g):
    return x * lax.rsqrt(jnp.mean(x * x, axis=-1, keepdims=True) + RMS_EPS) * g


def _pick_tile(n, pref):
    t = min(n, pref)
    while n % t:
        t //= 2
    return t


def _const_spec(shape):
    nd = len(shape)
    return pl.BlockSpec(shape, lambda *_: (0,) * nd, pipeline_mode=pl.Buffered(1))


def _params(sem):
    return pltpu.CompilerParams(dimension_semantics=sem, vmem_limit_bytes=VMEM_LIMIT)


def _in_proj_kernel(*refs, prompt, tm):
    if prompt:
        (x_ref, g_ref, wqkv_ref, wf_ref, bf_ref, wu_ref, wga_ref, wgs_ref, wqt_ref, wvt_ref, selk_ref,
         k_ref, v_ref, lf_ref, u_ref, sga_ref, sgs_ref, qat_ref, ka_ref, vat_ref, carry_ref) = refs
    else:
        (x_ref, g_ref, wqkv_ref, wf_ref, bf_ref, wu_ref, wga_ref, wgs_ref,
         q_ref, k_ref, v_ref, lf_ref, u_ref, sga_ref, sgs_ref) = refs

    hb = _rmsnorm(x_ref[...], g_ref[...]).astype(BF16)
    logf = jax.nn.log_sigmoid(_dot(hb, wf_ref[...]) + bf_ref[...])
    lf_ref[...] = logf[:, :N_HEADS]
    u = _dot(hb, wu_ref[...])
    for qd in range(u.shape[1] // LANES):
        u_ref[qd] = u[:, qd * LANES:(qd + 1) * LANES]
    sga_ref[...] = jax.nn.sigmoid(_dot(hb, wga_ref[...]))
    sgs_ref[...] = jax.nn.sigmoid(_dot(hb, wgs_ref[...]))
    if not prompt:
        qkv = _dot(hb, wqkv_ref[...])
        q_ref[...] = qkv[:, :D_ATTN]
        k_ref[...] = qkv[:, D_ATTN:2 * D_ATTN]
        v_ref[...] = qkv[:, 2 * D_ATTN:]
        return

    kv = _dot(hb, wqkv_ref[:, D_ATTN:])
    k = kv[:, :D_ATTN]
    k_ref[...] = k
    v_ref[...] = kv[:, D_ATTN:]

    @pl.when(pl.program_id(1) == 0)
    def _():
        carry_ref[...] = jnp.zeros_like(carry_ref)

    row = lax.broadcasted_iota(jnp.int32, (tm, tm), 0)
    col = lax.broadcasted_iota(jnp.int32, (tm, tm), 1)
    tri = jnp.where(row >= col, 1.0, 0.0).astype(BF16)
    l_hi, l_mid, l_lo = _split3(logf)
    c = _dot(tri, l_hi) + _dot(tri, l_mid) + _dot(tri, l_lo) + carry_ref[...]
    carry_ref[...] = c[tm - 1:tm, :]

    e = jnp.concatenate(_split3(c * LOG2E), axis=1)
    ek = _dot(e, selk_ref[...])
    lane = lax.broadcasted_iota(jnp.int32, (tm, LANES), 1)
    low = lane < HEAD_DIM
    for h in range(N_HEADS):
        t = h // 2
        kt = k[:, t * LANES:(t + 1) * LANES]
        if h % 2:
            kt = pltpu.roll(kt, HEAD_DIM, 1)
        ka_ref[h] = jnp.where(low, kt, ek[:, h * LANES:(h + 1) * LANES]).astype(BF16)
    qt = _dot_nt(wqt_ref[...], hb) * (SCALE * LOG2E)
    vt = _dot_nt(wvt_ref[...], hb)
    sub = lax.broadcasted_iota(jnp.int32, (HEAD_DIM, tm), 0)
    q_ones = jnp.where(sub < 3, 1.0, 0.0)
    v_ones = jnp.where(sub == 0, 1.0, 0.0)
    for h in range(N_HEADS):
        rows = slice(h * HEAD_DIM, (h + 1) * HEAD_DIM)
        qat_ref[h] = jnp.concatenate([qt[rows], q_ones], axis=0).astype(BF16)
        vat_ref[h] = jnp.concatenate([vt[rows], v_ones], axis=0).astype(BF16)


def _in_proj(x, g_mix, w, *, prompt, attn_blk=None):
    B, T, D = x.shape
    tm = _pick_tile(T, 256)
    nt = T // tm
    d_gate = w["wga"].shape[1]
    d_ssm = w["wu"].shape[1]
    row_spec = lambda n: pl.BlockSpec((None, tm, n), lambda b, t: (b, t, 0))
    ins = [x, g_mix, w["wqkv"], w["wf"], w["bf"], w["wu"], w["wga"], w["wgs"]]
    in_specs = [row_spec(D)] + [_const_spec(a.shape) for a in ins[1:]]
    outs = [jax.ShapeDtypeStruct((B, T, D_ATTN), F32),
            jax.ShapeDtypeStruct((B, T, D_ATTN), F32),
            jax.ShapeDtypeStruct((B, T, N_HEADS), F32),
            jax.ShapeDtypeStruct((B, d_ssm // LANES, T, LANES), F32),
            jax.ShapeDtypeStruct((B, T, d_gate), F32),
            jax.ShapeDtypeStruct((B, T, d_gate), F32)]
    out_specs = [row_spec(D_ATTN), row_spec(D_ATTN), row_spec(N_HEADS),
                 pl.BlockSpec((None, d_ssm // LANES, tm, LANES), lambda b, t: (b, 0, t, 0)),
                 row_spec(d_gate), row_spec(d_gate)]
    scratch = []
    if prompt:
        extra = [w["wqt"], w["wvt"], w["selk"]]
        ins += extra
        in_specs += [_const_spec(a.shape) for a in extra]
        per = attn_blk // tm
        outs += [jax.ShapeDtypeStruct((B, N_HEADS, LANES, T), BF16),
                 jax.ShapeDtypeStruct((B, N_HEADS, T, LANES), BF16),
                 jax.ShapeDtypeStruct((B, N_HEADS, T // attn_blk, LANES, attn_blk), BF16)]
        out_specs += [pl.BlockSpec((None, N_HEADS, LANES, tm), lambda b, t: (b, 0, 0, t)),
                      pl.BlockSpec((None, N_HEADS, tm, LANES), lambda b, t: (b, 0, t, 0)),
                      pl.BlockSpec((None, N_HEADS, None, LANES, tm), lambda b, t: (b, 0, t // per, 0, t % per))]
        scratch = [pltpu.VMEM((1, LANES), F32)]
    else:
        outs = [jax.ShapeDtypeStruct((B, T, D_ATTN), F32)] + outs
        out_specs = [row_spec(D_ATTN)] + out_specs
    return pl.pallas_call(
        functools.partial(_in_proj_kernel, prompt=prompt, tm=tm),
        grid=(B, nt), in_specs=in_specs, out_specs=out_specs, out_shape=outs, scratch_shapes=scratch,
        compiler_params=_params(("arbitrary", "arbitrary")),
        name="in_proj_prompt" if prompt else "in_proj_sample",
    )(*ins)


def _attn_prompt_kernel(qt_ref, k_ref, vt_ref, o_ref, acc0_ref, acc1_ref, *, bq, bk):
    qi = pl.program_id(2)
    ratio = bq // bk
    accs = (acc0_ref, acc1_ref)
    for a in accs:
        a[...] = jnp.zeros_like(a)

    def update(j, ms, diag):
        start = pl.multiple_of(j * bk, bk)
        q0 = 0 if diag is None else diag * bk
        new_ms = []
        for hh in range(2):
            st = _dot(k_ref[hh, pl.ds(start, bk), :], qt_ref[hh, :, q0:])
            if diag is not None:
                key = lax.broadcasted_iota(jnp.int32, st.shape, 0)
                qry = lax.broadcasted_iota(jnp.int32, st.shape, 1)
                st = jnp.where(key <= qry, st, NEG_INF)
            m_old = ms[hh][:, q0:]
            m_new = jnp.maximum(m_old, jnp.max(st, axis=0, keepdims=True))
            p = jnp.exp2(st - m_new).astype(BF16)
            accs[hh][:, q0:] = jnp.exp2(m_old - m_new) * accs[hh][:, q0:] + _dot(vt_ref[hh, j], p)
            new_ms.append(m_new if q0 == 0 else jnp.concatenate([ms[hh][:, :q0], m_new], axis=1))
        return tuple(new_ms)

    m0 = jnp.full((1, bq), NEG_INF, F32)
    ms = lax.fori_loop(0, qi * ratio, lambda j, ms: update(j, ms, None), (m0, m0))
    for d in range(ratio):
        ms = update(qi * ratio + d, ms, d)
    outs = []
    for a in accs:
        acc = a[...]
        outs.append(acc[:HEAD_DIM] / acc[HEAD_DIM:HEAD_DIM + 1])
    o_ref[...] = jnp.concatenate(outs, axis=0).T.astype(o_ref.dtype)


def _attn_prompt(qat, ka, vat):
    B, H, T, _ = ka.shape
    nk, bk = vat.shape[2], vat.shape[4]
    bq = _pick_tile(T, 4 * bk)
    return pl.pallas_call(
        functools.partial(_attn_prompt_kernel, bq=bq, bk=bk),
        grid=(B, H // 2, T // bq),
        in_specs=[pl.BlockSpec((None, 2, LANES, bq), lambda b, hp, qi: (b, hp, 0, qi)),
                  pl.BlockSpec((None, 2, T, LANES), lambda b, hp, qi: (b, hp, 0, 0)),
                  pl.BlockSpec((None, 2, nk, LANES, bk), lambda b, hp, qi: (b, hp, 0, 0, 0))],
        out_specs=pl.BlockSpec((None, bq, LANES), lambda b, hp, qi: (b, qi, hp)),
        out_shape=jax.ShapeDtypeStruct((B, T, D_ATTN), BF16),
        scratch_shapes=[pltpu.VMEM((LANES, bq), F32), pltpu.VMEM((LANES, bq), F32)],
        compiler_params=_params(("arbitrary", "arbitrary", "arbitrary")),
        name="attn_prompt",
    )(qat, ka, vat)


def _attn_sample_kernel(pt_ref, q_ref, kn_ref, vn_ref, lfn_ref, *rest, pg, tq):
    k_refs = rest[:pg]
    v_refs = rest[pg:2 * pg]
    lf_refs = rest[2 * pg:3 * pg]
    o_ref, qbd_ref, acc_ref, m_ref, l_ref, coff_ref = rest[3 * pg:]
    del pt_ref
    j = pl.program_id(1)
    hq = N_HEADS * tq

    @pl.when(j == 0)
    def _():
        q = q_ref[...] * LOG2E
        lane_head = lax.broadcasted_iota(jnp.int32, (tq, D_ATTN), 1) // HEAD_DIM
        qbd_ref[...] = jnp.concatenate(
            [jnp.where(lane_head == h, q, 0.0) for h in range(N_HEADS)], axis=0).astype(BF16)
        acc_ref[...] = jnp.zeros_like(acc_ref)
        l_ref[...] = jnp.zeros_like(l_ref)
        m_ref[...] = jnp.full_like(m_ref, NEG_INF)
        coff_ref[...] = jnp.zeros_like(coff_ref)

    r = lax.broadcasted_iota(jnp.int32, (PAGE_SIZE, PAGE_SIZE), 0)
    cc = lax.broadcasted_iota(jnp.int32, (PAGE_SIZE, PAGE_SIZE), 1)
    triu = jnp.where(r <= cc, 1.0, 0.0).astype(BF16)
    qbd = qbd_ref[...]

    def cum_logf(lf_list, coff):
        lf = jnp.concatenate(lf_list, axis=0) if len(lf_list) > 1 else lf_list[0]
        a, b, c3 = _split3(lf)
        w = _dot(a, triu) + _dot(b, triu) + _dot(c3, triu)
        pages = []
        for i in range(len(lf_list)):
            ck = w[i * N_HEADS:(i + 1) * N_HEADS] + coff
            pages.append(ck)
            coff = ck[:, PAGE_SIZE - 1:PAGE_SIZE]
        return pages, coff

    def page_scores(kt_page, ck):
        s = _dot(qbd, kt_page.astype(BF16))
        bias = jnp.concatenate(
            [jnp.broadcast_to(ck[h:h + 1, :], (tq, PAGE_SIZE)) for h in range(N_HEADS)], axis=0)
        return s - bias * LOG2E

    def online_update(s_list, vt_list):
        s = jnp.concatenate(s_list, axis=1) if len(s_list) > 1 else s_list[0]
        m_prev = m_ref[...]
        m_new = jnp.maximum(m_prev, jnp.max(s, axis=1, keepdims=True))
        p = jnp.exp2(s - m_new)
        alpha = jnp.exp2(m_prev - m_new)
        pv = None
        for i, vt_page in enumerate(vt_list):
            d = _dot_nt(p[:, i * PAGE_SIZE:(i + 1) * PAGE_SIZE].astype(BF16), vt_page.astype(BF16))
            pv = d if pv is None else pv + d
        acc_ref[...] = alpha * acc_ref[...] + pv
        l_ref[...] = alpha * l_ref[...] + jnp.sum(p, axis=1, keepdims=True)
        m_ref[...] = m_new

    cks, coff = cum_logf([lf_refs[i][...] for i in range(pg)], coff_ref[...])
    coff_ref[...] = coff
    online_update([page_scores(k_refs[i][...], cks[i]) for i in range(pg)], [v_refs[i][...] for i in range(pg)])

    @pl.when(j == pl.num_programs(1) - 1)
    def _():
        cnew, _unused = cum_logf([lfn_ref[...]], coff_ref[...])
        s = page_scores(kn_ref[...], cnew[0])
        key = lax.broadcasted_iota(jnp.int32, (hq, PAGE_SIZE), 1)
        qtok = lax.broadcasted_iota(jnp.int32, (hq, PAGE_SIZE), 0) % tq
        online_update([jnp.where(key <= qtok, s, NEG_INF)], [vn_ref[...]])
        o = acc_ref[...] / l_ref[...]
        lane = lax.broadcasted_iota(jnp.int32, (tq, LANES), 1)
        tiles = []
        for t in range(N_HEADS // 2):
            a = o[(2 * t) * tq:(2 * t + 1) * tq, t * LANES:(t + 1) * LANES]
            b = o[(2 * t + 1) * tq:(2 * t + 2) * tq, t * LANES:(t + 1) * LANES]
            tiles.append(jnp.where(lane < HEAD_DIM, a, b))
        o_ref[...] = jnp.concatenate(tiles, axis=1)


def _attn_sample(page_table, q, kt_new, vt_new, lft_new, cache_kt, cache_vt, cache_lft, layer):
    Bd, tq, _ = q.shape
    n_pages = page_table.shape[1]
    pg = _pick_tile(n_pages, 8)
    ng = n_pages // pg
    pt_flat = page_table.reshape(-1)

    def page_spec(rows, i):
        return pl.BlockSpec((None, None, rows, PAGE_SIZE),
                            lambda b, j, pt: (layer, pt[b * n_pages + j * pg + i], 0, 0))

    seq_spec = lambda shape: pl.BlockSpec((None,) + shape, lambda b, j, pt: (b, 0, 0))
    in_specs = ([seq_spec((tq, D_ATTN)), seq_spec((D_ATTN, PAGE_SIZE)), seq_spec((D_ATTN, PAGE_SIZE)),
                 seq_spec((N_HEADS, PAGE_SIZE))]
                + [page_spec(D_ATTN, i) for i in range(pg)] + [page_spec(D_ATTN, i) for i in range(pg)]
                + [page_spec(N_HEADS, i) for i in range(pg)])
    hq = N_HEADS * tq
    grid_spec = pltpu.PrefetchScalarGridSpec(
        num_scalar_prefetch=1, grid=(Bd, ng), in_specs=in_specs,
        out_specs=pl.BlockSpec((None, tq, D_ATTN), lambda b, j, pt: (b, 0, 0)),
        scratch_shapes=[pltpu.VMEM((hq, D_ATTN), BF16), pltpu.VMEM((hq, D_ATTN), F32),
                        pltpu.VMEM((hq, 1), F32), pltpu.VMEM((hq, 1), F32), pltpu.VMEM((N_HEADS, 1), F32)])
    return pl.pallas_call(
        functools.partial(_attn_sample_kernel, pg=pg, tq=tq),
        grid_spec=grid_spec,
        out_shape=jax.ShapeDtypeStruct((Bd, tq, D_ATTN), F32),
        compiler_params=_params(("arbitrary", "arbitrary")),
        name="attn_sample",
    )(pt_flat, q, kt_new, vt_new, lft_new, *([cache_kt] * pg), *([cache_vt] * pg), *([cache_lft] * pg))


def _s5_discretize(p):
    lr, li = p["lam_re"], p["lam_im"]
    dt = jnp.exp(p["log_dt"])[:, None]
    mag = jnp.exp(lr * dt)
    ang = li * dt
    ab_re = mag * jnp.cos(ang)
    ab_im = mag * jnp.sin(ang)
    den = lr * lr + li * li
    n_re = ab_re - 1.0
    n_im = ab_im
    f_re = (n_re * lr + n_im * li) / den
    f_im = (n_im * lr - n_re * li) / den
    bb_re = f_re[..., None] * p["b_re"] - f_im[..., None] * p["b_im"]
    bb_im = f_re[..., None] * p["b_im"] + f_im[..., None] * p["b_re"]
    return ab_re, ab_im, bb_re, bb_im


def _s5_block_weights(p):
    R = S5_BLOCK
    ab_re, ab_im, bb_re, bb_im = _s5_discretize(p)
    G, P, C = bb_re.shape
    pw_re, pw_im = [jnp.ones_like(ab_re)], [jnp.zeros_like(ab_im)]
    for _ in range(R):
        pr, pi = pw_re[-1], pw_im[-1]
        pw_re.append(pr * ab_re - pi * ab_im)
        pw_im.append(pr * ab_im + pi * ab_re)
    pw_re = jnp.stack(pw_re)
    pw_im = jnp.stack(pw_im)
    c_re, c_im = p["c_re"], p["c_im"]
    cp_re = c_re[None] * pw_re[:, :, None, :] - c_im[None] * pw_im[:, :, None, :]
    cp_im = c_re[None] * pw_im[:, :, None, :] + c_im[None] * pw_re[:, :, None, :]
    m = (jnp.einsum('jgcp,gpd->jgdc', cp_re[:R], bb_re, precision=HI)
         - jnp.einsum('jgcp,gpd->jgdc', cp_im[:R], bb_im, precision=HI))
    lag = jnp.arange(R)[None, :] - jnp.arange(R)[:, None]
    kt = jnp.where((lag >= 0)[:, :, None, None, None], m[jnp.clip(lag, 0, R - 1)], 0.0)
    kt = kt.transpose(2, 0, 3, 1, 4).reshape(G, R * C, R * C)
    pb_re = pw_re[R - 1::-1][:R]
    pb_im = pw_im[R - 1::-1][:R]
    x_re = pb_re[..., None] * bb_re[None] - pb_im[..., None] * bb_im[None]
    x_im = pb_re[..., None] * bb_im[None] + pb_im[..., None] * bb_re[None]
    x_re = x_re.transpose(1, 0, 3, 2).reshape(G, R * C, P)
    x_im = x_im.transpose(1, 0, 3, 2).reshape(G, R * C, P)
    wx = jnp.concatenate([x_re, x_im, x_im, x_re], axis=-1)
    wc_re = cp_re[1:].transpose(1, 3, 0, 2).reshape(G, P, R * C)
    wc_im = cp_im[1:].transpose(1, 3, 0, 2).reshape(G, P, R * C)
    wc = jnp.concatenate([wc_re, -wc_im], axis=1)
    a_blk = jnp.concatenate([pw_re[R], pw_re[R]], axis=-1)
    b_blk = jnp.concatenate([-pw_im[R], pw_im[R]], axis=-1)
    d = jnp.tile(p["d_skip"].reshape(G, 1, C), (1, 1, R))
    return kt.astype(BF16), wx.astype(BF16), wc.astype(BF16), a_blk, b_blk, d


def _s5_prompt_kernel(u_ref, kt_ref, wx_ref, wc_ref, a_ref, b_ref, d_ref, y_ref, hfin_ref,
                      ug_ref, xa_ref, xb_ref, hp_ref, h_ref, hsw_ref, *, tb, ng):
    R, C = S5_BLOCK, SSM_GROUP
    gpt = LANES // C
    sp = 2 * STATE_P

    @pl.when(pl.program_id(1) == 0)
    def _():
        h_ref[...] = jnp.zeros_like(h_ref)
        hsw_ref[...] = jnp.zeros_like(hsw_ref)

    lane_chunk = lax.broadcasted_iota(jnp.int32, (tb, LANES), 1) // C

    def chunk_transpose(vs):
        vs = list(vs)
        step = gpt // 2
        while step >= 1:
            low = (lane_chunk // step) % 2 == 0
            for i in range(gpt):
                if (i // step) % 2 == 0:
                    a, b = vs[i], vs[i + step]
                    vs[i] = jnp.where(low, a, pltpu.roll(b, step * C, 1))
                    vs[i + step] = jnp.where(low, pltpu.roll(a, LANES - step * C, 1), b)
            step //= 2
        return vs

    for qd in range(ng // gpt):
        for half in range(R // gpt):
            toks = [u_ref[qd, pl.ds(half * gpt + r8, tb, stride=R), :] for r8 in range(gpt)]
            for gl, v in enumerate(chunk_transpose(toks)):
                ug_ref[qd * gpt + gl, :, half * LANES:(half + 1) * LANES] = v

    def xbody(g, carry):
        x = _dot(ug_ref[g].astype(BF16), wx_ref[g])
        rows = pl.ds(pl.multiple_of(g * tb, tb), tb)
        xa_ref[rows, :] = x[:, :sp]
        xb_ref[rows, :] = x[:, sp:]
        return carry

    lax.fori_loop(0, ng, xbody, 0)

    a = a_ref[...]
    b = b_ref[...]

    def sbody(k, carry):
        h, hsw = carry
        rows = pl.ds(k, ng, stride=tb)
        hp_ref[rows, :] = h
        return a * h + b * hsw + xa_ref[rows, :], a * hsw - b * h + xb_ref[rows, :]

    h, hsw = lax.fori_loop(0, tb, sbody, (h_ref[...], hsw_ref[...]))
    h_ref[...] = h
    hsw_ref[...] = hsw
    hfin_ref[...] = h

    def ybody(g, carry):
        u = ug_ref[g]
        hp = hp_ref[pl.ds(pl.multiple_of(g * tb, tb), tb), :].astype(BF16)
        ug_ref[g] = _dot(u.astype(BF16), kt_ref[g]) + _dot(hp, wc_ref[g]) + u * d_ref[g]
        return carry

    lax.fori_loop(0, ng, ybody, 0)

    for qd in range(ng // gpt):
        for half in range(R // gpt):
            grps = [ug_ref[qd * gpt + gl, :, half * LANES:(half + 1) * LANES] for gl in range(gpt)]
            for r8, v in enumerate(chunk_transpose(grps)):
                y_ref[qd, pl.ds(half * gpt + r8, tb, stride=R), :] = v


def _s5_prompt(u, weights):
    kt, wx, wc, a_blk, b_blk, d = weights
    B, nq, T, _ = u.shape
    G = kt.shape[0]
    R = S5_BLOCK
    tb = _pick_tile(T // R, 128)
    sp = 2 * STATE_P
    blk_spec = pl.BlockSpec((None, nq, tb * R, LANES), lambda b, t: (b, 0, t, 0))
    return pl.pallas_call(
        functools.partial(_s5_prompt_kernel, tb=tb, ng=G),
        grid=(B, T // (tb * R)),
        in_specs=[blk_spec] + [_const_spec(x.shape) for x in (kt, wx, wc, a_blk, b_blk, d)],
        out_specs=[blk_spec, pl.BlockSpec((None, G, sp), lambda b, t: (b, 0, 0))],
        out_shape=[jax.ShapeDtypeStruct(u.shape, F32), jax.ShapeDtypeStruct((B, G, sp), F32)],
        scratch_shapes=[pltpu.VMEM((G, tb, R * SSM_GROUP), F32),
                        pltpu.VMEM((G * tb, sp), F32), pltpu.VMEM((G * tb, sp), F32), pltpu.VMEM((G * tb, sp), F32),
                        pltpu.VMEM((G, sp), F32), pltpu.VMEM((G, sp), F32)],
        compiler_params=_params(("arbitrary", "arbitrary")),
        name="s5_prompt",
    )(u, kt, wx, wc, a_blk, b_blk, d)


def _s5_sample_kernel(u_ref, h0_ref, bbd_ref, cmat_ref, are_ref, aim_ref, d_ref, y_ref, ht_ref, hs_ref, *, nt, nb):
    ns = are_ref.shape[1]
    u = u_ref[...]
    ub = u.astype(BF16)
    cw = 256
    for c0 in range(0, ns, cw):
        x_re = _dot(ub, bbd_ref[:, c0:c0 + cw])
        x_im = _dot(ub, bbd_ref[:, ns + c0:ns + c0 + cw])
        are = are_ref[:, c0:c0 + cw]
        aim = aim_ref[:, c0:c0 + cw]
        hr = h0_ref[:, c0:c0 + cw]
        hi = h0_ref[:, ns + c0:ns + c0 + cw]
        for t in range(nt):
            rows = slice(t * nb, (t + 1) * nb)
            hr, hi = are * hr - aim * hi + x_re[rows], are * hi + aim * hr + x_im[rows]
            hs_ref[rows, c0:c0 + cw] = hr.astype(BF16)
            hs_ref[rows, ns + c0:ns + c0 + cw] = hi.astype(BF16)
        ht_ref[:, c0:c0 + cw] = hr
        ht_ref[:, ns + c0:ns + c0 + cw] = hi
    y_ref[...] = _dot(hs_ref[...], cmat_ref[...]) + u * d_ref[...]


def _s5_sample(u_tm, h0, p, nt):
    ab_re, ab_im, bb_re, bb_im = _s5_discretize(p)
    G, P, C = bb_re.shape
    eye = jnp.eye(G, dtype=F32)
    bd = lambda w: jnp.einsum('gpc,gk->gckp', w, eye).reshape(G * C, G * P)
    bbd = jnp.concatenate([bd(bb_re), bd(bb_im)], axis=1).astype(BF16)
    cd = lambda w: jnp.einsum('gcp,gk->gpkc', w, eye).reshape(G * P, G * C)
    cmat = jnp.concatenate([cd(p["c_re"]), -cd(p["c_im"])], axis=0).astype(BF16)
    nb = h0.shape[0]
    ins = [u_tm, h0, bbd, cmat, ab_re.reshape(1, G * P), ab_im.reshape(1, G * P), p["d_skip"].reshape(1, G * C)]
    return pl.pallas_call(
        functools.partial(_s5_sample_kernel, nt=nt, nb=nb),
        grid=(1,),
        in_specs=[_const_spec(x.shape) for x in ins],
        out_specs=[pl.BlockSpec(u_tm.shape, lambda i: (0, 0)), pl.BlockSpec(h0.shape, lambda i: (0, 0))],
        out_shape=[jax.ShapeDtypeStruct(u_tm.shape, F32), jax.ShapeDtypeStruct(h0.shape, F32)],
        scratch_shapes=[pltpu.VMEM((nt * nb, 2 * G * P), BF16)],
        compiler_params=_params(("arbitrary",)),
        name="s5_sample",
    )(*ins)


def _mix_ffn_kernel(x_ref, s_ref, attn_ref, sga_ref, sgs_ref, prev_ref,
                    wa_ref, wb_ref, wao_ref, wout_ref, gffn_ref, wg_ref, wup_ref, wconv_ref, bconv_ref, wd_ref,
                    gfin_ref, y_ref, conv_ref, gate_ref, *, tm, shift, pad):
    hist = (CONV_W - 1) * shift

    @pl.when(pl.program_id(1) == 0)
    def _():
        gate_ref[pad - hist:pad, :] = prev_ref[...]

    s = jnp.concatenate([s_ref[qd] for qd in range(s_ref.shape[0])], axis=1)
    zb = jax.nn.gelu(s).astype(BF16)
    ssm_out = _dot(zb, wa_ref[...]) * jax.nn.sigmoid(_dot(zb, wb_ref[...]))
    attn_out = _dot(attn_ref[...], wao_ref[...])
    merged = sgs_ref[...] * ssm_out + sga_ref[...] * attn_out
    x1 = x_ref[...] + _dot(merged.astype(BF16), wout_ref[...])
    h2 = _rmsnorm(x1, gffn_ref[...]).astype(BF16)
    gate_ref[pad:pad + tm, :] = _dot(h2, wg_ref[...])
    gc = gate_ref[pad - hist:pad - hist + tm, :] * wconv_ref[0:1, :]
    for i in range(1, CONV_W):
        gc = gc + gate_ref[pad - hist + i * shift:pad - hist + i * shift + tm, :] * wconv_ref[i:i + 1, :]
    gc = bconv_ref[...] + gc
    act = (jax.nn.gelu(gc) * _dot(h2, wup_ref[...])).astype(BF16)
    x2 = x1 + _dot(act, wd_ref[...])
    y_ref[...] = _rmsnorm(x2, gfin_ref[...])
    tail = gate_ref[pad + tm - hist:pad + tm, :]
    conv_ref[...] = tail
    gate_ref[pad - hist:pad, :] = tail


def _mix_ffn(x, s, attn, sga, sgs, prev, w, *, shift, tm_pref):
    B, T, D = x.shape
    tm = _pick_tile(T, tm_pref)
    hist = (CONV_W - 1) * shift
    pad = -(-hist // 8) * 8
    F = w["wg"].shape[1]
    row_spec = lambda n: pl.BlockSpec((None, tm, n), lambda b, t: (b, t, 0))
    seq_spec = pl.BlockSpec((None, hist, F), lambda b, t: (b, 0, 0))
    consts = [w["wa"], w["wb"], w["wao"], w["wout"], w["gffn"], w["wg"], w["wup"], w["wconv"], w["bconv"], w["wd"],
              w["gfin"]]
    return pl.pallas_call(
        functools.partial(_mix_ffn_kernel, tm=tm, shift=shift, pad=pad),
        grid=(B, T // tm),
        in_specs=[row_spec(D), pl.BlockSpec((None, s.shape[1], tm, LANES), lambda b, t: (b, 0, t, 0)),
                  row_spec(attn.shape[2]), row_spec(D), row_spec(D), seq_spec]
        + [_const_spec(a.shape) for a in consts],
        out_specs=[row_spec(D), seq_spec],
        out_shape=[jax.ShapeDtypeStruct((B, T, D), F32), jax.ShapeDtypeStruct((B, hist, F), F32)],
        scratch_shapes=[pltpu.VMEM((pad + tm, F), F32)],
        compiler_params=_params(("arbitrary", "arbitrary")),
        name="mix_ffn_s%d" % shift,
    )(x, s, attn, sga, sgs, prev, *consts)


def _pack_weights(l, g_mix, w_in, b_f, w_glu_a, w_glu_b, w_attn_out, w_out, g_ffn, w_gate, w_up, w_conv, b_conv,
                  w_down, g_final):
    D = w_in.shape[1]
    d_ssm = w_glu_a.shape[1]
    o = [0, D_ATTN, 2 * D_ATTN, 3 * D_ATTN, 3 * D_ATTN + N_HEADS, 3 * D_ATTN + N_HEADS + d_ssm]
    wi = w_in[l]
    wq = wi[:, o[0]:o[1]] * SCALE
    w = {
        "wqkv": jnp.concatenate([wq, wi[:, o[1]:o[3]]], axis=1).astype(BF16),
        "wf": jnp.pad(wi[:, o[3]:o[4]], ((0, 0), (0, LANES - N_HEADS))).astype(BF16),
        "bf": jnp.pad(b_f[l], (0, LANES - N_HEADS)).reshape(1, LANES),
        "wu": wi[:, o[4]:o[5]].astype(BF16),
        "wga": wi[:, o[5]:o[5] + D].astype(BF16),
        "wgs": wi[:, o[5] + D:o[5] + 2 * D].astype(BF16),
        "wa": w_glu_a[l].astype(BF16), "wb": w_glu_b[l].astype(BF16), "wao": w_attn_out[l].astype(BF16),
        "wout": w_out[l].astype(BF16), "gffn": g_ffn[l].reshape(1, D), "wg": w_gate[l].astype(BF16),
        "wup": w_up[l].astype(BF16), "wconv": w_conv[l], "bconv": b_conv[l].reshape(1, -1),
        "wd": w_down[l].astype(BF16), "gfin": g_final.reshape(1, D),
    }
    w["wqt"] = wi[:, o[0]:o[1]].T.astype(BF16)
    w["wvt"] = wi[:, o[2]:o[3]].T.astype(BF16)
    e = jnp.arange(3 * LANES)
    part, head = e // LANES, e % LANES
    colk = head * LANES + HEAD_DIM + part
    cols = jnp.arange(N_HEADS * LANES)[None, :]
    w["selk"] = jnp.where((head < N_HEADS)[:, None] & (cols == colk[:, None]), -1.0, 0.0).astype(BF16)
    return w


def kernel(x_prompt, x_sample, cache_k, cache_v, cache_logf, page_table, state_ssm_re, state_ssm_im, state_conv,
           g_mix, w_in, b_f, lam_re, lam_im, log_dt, b_re, b_im, c_re, c_im, d_skip, w_glu_a, w_glu_b,
           w_attn_out, w_out, g_ffn, w_gate, w_up, w_conv, b_conv, w_down, g_final):
    depth = w_in.shape[0]
    assert depth == 1, "final norm is fused into the layer kernel; one layer supported"
    l = 0
    Bp, T, D = x_prompt.shape
    Bd, Td, _ = x_sample.shape
    n_pool = cache_k.shape[1]
    G, P = lam_re.shape[1], lam_re.shape[2]
    C = SSM_GROUP
    F = w_gate.shape[2]
    R = S5_BLOCK
    assert T % R == 0

    w = _pack_weights(l, g_mix, w_in, b_f, w_glu_a, w_glu_b, w_attn_out, w_out, g_ffn, w_gate, w_up, w_conv,
                      b_conv, w_down, g_final)
    gm = g_mix[l].reshape(1, D)
    sp = {"lam_re": lam_re[l], "lam_im": lam_im[l], "log_dt": log_dt[l], "b_re": b_re[l], "b_im": b_im[l],
          "c_re": c_re[l], "c_im": c_im[l], "d_skip": d_skip[l]}

    k_p, v_p, lf_p, u_p, sga_p, sgs_p, qat, ka, vat = _in_proj(x_prompt, gm, w, prompt=True,
                                                                 attn_blk=_pick_tile(T, 512))
    attn_p = _attn_prompt(qat, ka, vat)
    s_p, hfin = _s5_prompt(u_p, _s5_block_weights(sp))
    zeros_c = jnp.zeros((Bp, CONV_W - 1, F), F32)
    y_p, conv_p = _mix_ffn(x_prompt, s_p, attn_p, sga_p, sgs_p, zeros_c, w, shift=1, tm_pref=256)

    xs_tm = x_sample.transpose(1, 0, 2).reshape(1, Td * Bd, D)
    q_s, k_s, v_s, lf_s, u_s, sga_s, sgs_s = _in_proj(xs_tm, gm, w, prompt=False)
    to_bt = lambda a: a.reshape(Td, Bd, a.shape[-1]).transpose(1, 0, 2)
    q_b, k_b, v_b, lf_b = to_bt(q_s), to_bt(k_s), to_bt(v_s), to_bt(lf_s)
    tpad = lambda a: jnp.pad(a.transpose(0, 2, 1), ((0, 0), (0, 0), (0, PAGE_SIZE - Td)))
    page_t = lambda c: c.transpose(0, 1, 3, 4, 2).reshape(depth, n_pool, D_ATTN, PAGE_SIZE)
    attn_s = _attn_sample(page_table, q_b, tpad(k_b), tpad(v_b), tpad(lf_b), page_t(cache_k), page_t(cache_v),
                          cache_logf.transpose(0, 1, 3, 2), l)
    h0 = jnp.concatenate([state_ssm_re[l].reshape(Bd, G * P), state_ssm_im[l].reshape(Bd, G * P)], axis=1)
    tiles = lambda a: a.reshape(a.shape[0], -1, LANES).transpose(1, 0, 2)
    s_s, ht = _s5_sample(u_s[0].transpose(1, 0, 2).reshape(Td * Bd, G * C), h0, sp, Td)
    attn_s_tm = attn_s.transpose(1, 0, 2).reshape(1, Td * Bd, D_ATTN).astype(BF16)
    prev_s = state_conv[l].transpose(1, 0, 2).reshape(1, (CONV_W - 1) * Bd, F)
    y_s, conv_s = _mix_ffn(xs_tm, tiles(s_s)[None], attn_s_tm, sga_s, sgs_s, prev_s, w,
                           shift=Bd, tm_pref=256)

    y_sample = y_s.reshape(Td, Bd, D).transpose(1, 0, 2)
    conv_sample = conv_s.reshape(CONV_W - 1, Bd, F).transpose(1, 0, 2)
    hd = lambda a, b, t: a.reshape(1, b, t, N_HEADS, HEAD_DIM)
    return (y_p, y_sample,
            hd(k_p, Bp, T), hd(v_p, Bp, T), lf_p.reshape(1, Bp, T, N_HEADS),
            hd(k_b, Bd, Td), hd(v_b, Bd, Td), lf_b.reshape(1, Bd, Td, N_HEADS),
            hfin[:, :, :P].reshape(1, Bp, G, P), hfin[:, :, P:].reshape(1, Bp, G, P),
            ht[:, :G * P].reshape(1, Bd, G, P), ht[:, G * P:].reshape(1, Bd, G, P),
            conv_p.reshape(1, Bp, CONV_W - 1, F), conv_sample.reshape(1, Bd, CONV_W - 1, F))
```

```python
import functools

import jax
import jax.numpy as jnp
from jax import lax
from jax.experimental import pallas as pl
from jax.experimental.pallas import tpu as pltpu

N_HEADS = 8
HEAD_DIM = 64
D_ATTN = N_HEADS * HEAD_DIM
SSM_GROUP = 16
STATE_P = 64
CONV_W = 3
PAGE_SIZE = 128
RMS_EPS = 1e-6
NEG_INF = -1e30
SCALE = HEAD_DIM ** -0.5
LOG2E = 1.4426950408889634

LANES = 128
VT_ROWS = 80
S5_BLOCK = 16
VMEM_LIMIT = 56 * 1024 * 1024

F32 = jnp.float32
BF16 = jnp.bfloat16
HI = lax.Precision.HIGHEST


def _dot(a, b):
    return jnp.dot(a, b, preferred_element_type=F32)


def _dot_nt(a, b):
    return lax.dot_general(a, b, (((1,), (1,)), ((), ())), preferred_element_type=F32)


def _split3(x):
    hi = x.astype(BF16)
    r1 = x - hi.astype(F32)
    mid = r1.astype(BF16)
    lo = (r1 - mid.astype(F32)).astype(BF16)
    return hi, mid, lo


def _rmsnorm(x, g):
    return x * lax.rsqrt(jnp.mean(x * x, axis=-1, keepdims=True) + RMS_EPS) * g


def _pick_tile(n, pref):
    t = min(n, pref)
    while n % t:
        t //= 2
    return t


def _const_spec(shape):
    nd = len(shape)
    return pl.BlockSpec(shape, lambda *_: (0,) * nd, pipeline_mode=pl.Buffered(1))


def _params(sem):
    return pltpu.CompilerParams(dimension_semantics=sem, vmem_limit_bytes=VMEM_LIMIT)


def _in_proj_kernel(*refs, prompt, tm):
    if prompt:
        (x_ref, g_ref, wqkv_ref, wf_ref, bf_ref, wu_ref, wga_ref, wgs_ref, wqt_ref, wvt_ref, selk_ref,
         k_ref, v_ref, lf_ref, u_ref, sga_ref, sgs_ref, qat_ref, ka_ref, vat_ref, carry_ref) = refs
    else:
        (x_ref, g_ref, wqkv_ref, wf_ref, bf_ref, wu_ref, wga_ref, wgs_ref,
         q_ref, k_ref, v_ref, lf_ref, u_ref, sga_ref, sgs_ref) = refs

    hb = _rmsnorm(x_ref[...], g_ref[...]).astype(BF16)
    logf = jax.nn.log_sigmoid(_dot(hb, wf_ref[...]) + bf_ref[...])
    lf_ref[...] = logf[:, :N_HEADS]
    u = _dot(hb, wu_ref[...])
    for qd in range(u.shape[1] // LANES):
        u_ref[qd] = u[:, qd * LANES:(qd + 1) * LANES]
    sga_ref[...] = jax.nn.sigmoid(_dot(hb, wga_ref[...]))
    sgs_ref[...] = jax.nn.sigmoid(_dot(hb, wgs_ref[...]))
    if not prompt:
        qkv = _dot(hb, wqkv_ref[...])
        q_ref[...] = qkv[:, :D_ATTN]
        k_ref[...] = qkv[:, D_ATTN:2 * D_ATTN]
        v_ref[...] = qkv[:, 2 * D_ATTN:]
        return

    k = _dot(hb, wqkv_ref[:, D_ATTN:2 * D_ATTN])
    k_ref[...] = k

    @pl.when(pl.program_id(1) == 0)
    def _():
        carry_ref[...] = jnp.zeros_like(carry_ref)

    row = lax.broadcasted_iota(jnp.int32, (tm, tm), 0)
    col = lax.broadcasted_iota(jnp.int32, (tm, tm), 1)
    tri = jnp.where(row >= col, 1.0, 0.0).astype(BF16)
    l_hi, l_mid, l_lo = _split3(logf)
    c = _dot(tri, l_hi) + _dot(tri, l_mid) + _dot(tri, l_lo) + carry_ref[...]
    carry_ref[...] = c[tm - 1:tm, :]

    e = jnp.concatenate(_split3(c * LOG2E), axis=1)
    ek = _dot(e, selk_ref[...])
    lane = lax.broadcasted_iota(jnp.int32, (tm, LANES), 1)
    low = lane < HEAD_DIM
    for h in range(N_HEADS):
        t = h // 2
        kt = k[:, t * LANES:(t + 1) * LANES]
        if h % 2:
            kt = pltpu.roll(kt, HEAD_DIM, 1)
        ka_ref[h] = jnp.where(low, kt, ek[:, h * LANES:(h + 1) * LANES]).astype(BF16)
    qt = _dot_nt(wqt_ref[...], hb) * (SCALE * LOG2E)
    vt = _dot_nt(wvt_ref[...], hb)
    v_ref[...] = vt
    sub = lax.broadcasted_iota(jnp.int32, (HEAD_DIM, tm), 0)
    q_ones = jnp.where(sub < 3, 1.0, 0.0)
    sub_v = lax.broadcasted_iota(jnp.int32, (VT_ROWS - HEAD_DIM, tm), 0)
    v_ones = jnp.where(sub_v == 0, 1.0, 0.0)
    for h in range(N_HEADS):
        rows = slice(h * HEAD_DIM, (h + 1) * HEAD_DIM)
        qat_ref[h] = jnp.concatenate([qt[rows], q_ones], axis=0).astype(BF16)
        vat_ref[h] = jnp.concatenate([vt[rows], v_ones], axis=0).astype(BF16)


def _in_proj(x, g_mix, w, *, prompt, attn_blk=None):
    B, T, D = x.shape
    tm = _pick_tile(T, 256)
    nt = T // tm
    d_gate = w["wga"].shape[1]
    d_ssm = w["wu"].shape[1]
    row_spec = lambda n: pl.BlockSpec((None, tm, n), lambda b, t: (b, t, 0))
    ins = [x, g_mix, w["wqkv"], w["wf"], w["bf"], w["wu"], w["wga"], w["wgs"]]
    in_specs = [row_spec(D)] + [_const_spec(a.shape) for a in ins[1:]]
    outs = [jax.ShapeDtypeStruct((B, T, D_ATTN), F32),
            jax.ShapeDtypeStruct((B, T, D_ATTN), F32),
            jax.ShapeDtypeStruct((B, T, N_HEADS), F32),
            jax.ShapeDtypeStruct((B, d_ssm // LANES, T, LANES), F32),
            jax.ShapeDtypeStruct((B, T, d_gate), F32),
            jax.ShapeDtypeStruct((B, T, d_gate), F32)]
    out_specs = [row_spec(D_ATTN), row_spec(D_ATTN), row_spec(N_HEADS),
                 pl.BlockSpec((None, d_ssm // LANES, tm, LANES), lambda b, t: (b, 0, t, 0)),
                 row_spec(d_gate), row_spec(d_gate)]
    scratch = []
    if prompt:
        outs[1] = jax.ShapeDtypeStruct((B, D_ATTN, T), F32)
        out_specs[1] = pl.BlockSpec((None, D_ATTN, tm), lambda b, t: (b, 0, t))
        extra = [w["wqt"], w["wvt"], w["selk"]]
        ins += extra
        in_specs += [_const_spec(a.shape) for a in extra]
        per = attn_blk // tm
        outs += [jax.ShapeDtypeStruct((B, N_HEADS, LANES, T), BF16),
                 jax.ShapeDtypeStruct((B, N_HEADS, T, LANES), BF16),
                 jax.ShapeDtypeStruct((B, N_HEADS, T // attn_blk, VT_ROWS, attn_blk), BF16)]
        out_specs += [pl.BlockSpec((None, N_HEADS, LANES, tm), lambda b, t: (b, 0, 0, t)),
                      pl.BlockSpec((None, N_HEADS, tm, LANES), lambda b, t: (b, 0, t, 0)),
                      pl.BlockSpec((None, N_HEADS, None, VT_ROWS, tm), lambda b, t: (b, 0, t // per, 0, t % per))]
        scratch = [pltpu.VMEM((1, LANES), F32)]
    else:
        outs = [jax.ShapeDtypeStruct((B, T, D_ATTN), F32)] + outs
        out_specs = [row_spec(D_ATTN)] + out_specs
    return pl.pallas_call(
        functools.partial(_in_proj_kernel, prompt=prompt, tm=tm),
        grid=(B, nt), in_specs=in_specs, out_specs=out_specs, out_shape=outs, scratch_shapes=scratch,
        compiler_params=_params(("arbitrary", "arbitrary")),
        name="in_proj_prompt" if prompt else "in_proj_sample",
    )(*ins)


def _attn_prompt_kernel(qt_ref, k_ref, vt_ref, o_ref, acc0_ref, acc1_ref, *, bq, bk):
    qi = pl.program_id(2)
    ratio = bq // bk
    accs = (acc0_ref, acc1_ref)
    for a in accs:
        a[...] = jnp.zeros_like(a)

    def update(j, ms, diag):
        start = pl.multiple_of(j * bk, bk)
        q0 = 0 if diag is None else diag * bk
        new_ms = []
        for hh in range(2):
            st = _dot(k_ref[hh, pl.ds(start, bk), :], qt_ref[hh, :, q0:])
            if diag is not None:
                key = lax.broadcasted_iota(jnp.int32, st.shape, 0)
                qry = lax.broadcasted_iota(jnp.int32, st.shape, 1)
                st = jnp.where(key <= qry, st, NEG_INF)
            m_old = ms[hh][:, q0:]
            m_new = jnp.maximum(m_old, jnp.max(st, axis=0, keepdims=True))
            p = jnp.exp2(st - m_new).astype(BF16)
            accs[hh][:, q0:] = jnp.exp2(m_old - m_new) * accs[hh][:, q0:] + _dot(vt_ref[hh, j], p)
            new_ms.append(m_new if q0 == 0 else jnp.concatenate([ms[hh][:, :q0], m_new], axis=1))
        return tuple(new_ms)

    m0 = jnp.full((1, bq), NEG_INF, F32)
    ms = lax.fori_loop(0, qi * ratio, lambda j, ms: update(j, ms, None), (m0, m0))
    for d in range(ratio):
        ms = update(qi * ratio + d, ms, d)
    outs = []
    for a in accs:
        acc = a[...]
        outs.append(acc[:HEAD_DIM] / acc[HEAD_DIM:HEAD_DIM + 1])
    o_ref[...] = jnp.concatenate(outs, axis=0).T.astype(o_ref.dtype)


def _attn_prompt(qat, ka, vat):
    B, H, T, _ = ka.shape
    nk, bk = vat.shape[2], vat.shape[4]
    bq = _pick_tile(T, 4 * bk)
    return pl.pallas_call(
        functools.partial(_attn_prompt_kernel, bq=bq, bk=bk),
        grid=(B, H // 2, T // bq),
        in_specs=[pl.BlockSpec((None, 2, LANES, bq), lambda b, hp, qi: (b, hp, 0, qi)),
                  pl.BlockSpec((None, 2, T, LANES), lambda b, hp, qi: (b, hp, 0, 0)),
                  pl.BlockSpec((None, 2, nk, VT_ROWS, bk), lambda b, hp, qi: (b, hp, 0, 0, 0))],
        out_specs=pl.BlockSpec((None, bq, LANES), lambda b, hp, qi: (b, qi, hp)),
        out_shape=jax.ShapeDtypeStruct((B, T, D_ATTN), BF16),
        scratch_shapes=[pltpu.VMEM((VT_ROWS, bq), F32), pltpu.VMEM((VT_ROWS, bq), F32)],
        compiler_params=_params(("arbitrary", "arbitrary", "arbitrary")),
        name="attn_prompt",
    )(qat, ka, vat)


def _attn_sample_kernel(pt_ref, q_ref, kn_ref, vn_ref, lfn_ref, *rest, pg, tq):
    k_refs = rest[:pg]
    v_refs = rest[pg:2 * pg]
    lf_refs = rest[2 * pg:3 * pg]
    o_ref, qbd_ref, acc_ref, m_ref, l_ref, coff_ref = rest[3 * pg:]
    del pt_ref
    j = pl.program_id(1)
    hq = N_HEADS * tq

    @pl.when(j == 0)
    def _():
        q = q_ref[...] * LOG2E
        lane_head = lax.broadcasted_iota(jnp.int32, (tq, D_ATTN), 1) // HEAD_DIM
        qbd_ref[...] = jnp.concatenate(
            [jnp.where(lane_head == h, q, 0.0) for h in range(N_HEADS)], axis=0).astype(BF16)
        acc_ref[...] = jnp.zeros_like(acc_ref)
        l_ref[...] = jnp.zeros_like(l_ref)
        m_ref[...] = jnp.full_like(m_ref, NEG_INF)
        coff_ref[...] = jnp.zeros_like(coff_ref)

    r = lax.broadcasted_iota(jnp.int32, (PAGE_SIZE, PAGE_SIZE), 0)
    cc = lax.broadcasted_iota(jnp.int32, (PAGE_SIZE, PAGE_SIZE), 1)
    triu = jnp.where(r <= cc, 1.0, 0.0).astype(BF16)
    qbd = qbd_ref[...]

    def cum_logf(lf_list, coff):
        n = len(lf_list)
        lf = jnp.concatenate(lf_list, axis=0) if n > 1 else lf_list[0]
        a, b, c3 = _split3(lf)
        w = _dot(a, triu) + _dot(b, triu) + _dot(c3, triu)
        tot = jnp.broadcast_to(w[:, PAGE_SIZE - 1:PAGE_SIZE], w.shape)
        pages = []
        for i in range(n):
            pages.append(w[i * N_HEADS:(i + 1) * N_HEADS] + coff)
            coff = coff + tot[i * N_HEADS:(i + 1) * N_HEADS]
        return pages, coff

    def page_scores(kt_page, ck):
        s = _dot(qbd, kt_page.astype(BF16))
        bias = jnp.concatenate(
            [jnp.broadcast_to(ck[h:h + 1, :], (tq, PAGE_SIZE)) for h in range(N_HEADS)], axis=0)
        return s - bias * LOG2E

    def online_update(s_list, vt_list):
        s = jnp.concatenate(s_list, axis=1) if len(s_list) > 1 else s_list[0]
        m_prev = m_ref[...]
        m_new = jnp.maximum(m_prev, jnp.max(s, axis=1, keepdims=True))
        p = jnp.exp2(s - m_new)
        alpha = jnp.exp2(m_prev - m_new)
        pv = None
        for i, vt_page in enumerate(vt_list):
            d = _dot_nt(p[:, i * PAGE_SIZE:(i + 1) * PAGE_SIZE].astype(BF16), vt_page.astype(BF16))
            pv = d if pv is None else pv + d
        acc_ref[...] = alpha * acc_ref[...] + pv
        l_ref[...] = alpha * l_ref[...] + jnp.sum(p, axis=1, keepdims=True)
        m_ref[...] = m_new

    cks, coff = cum_logf([lf_refs[i][...] for i in range(pg)], coff_ref[...])
    coff_ref[...] = coff
    online_update([page_scores(k_refs[i][...], cks[i]) for i in range(pg)], [v_refs[i][...] for i in range(pg)])

    @pl.when(j == pl.num_programs(1) - 1)
    def _():
        cnew, _unused = cum_logf([lfn_ref[...]], coff_ref[...])
        s = page_scores(kn_ref[...], cnew[0])
        key = lax.broadcasted_iota(jnp.int32, (hq, PAGE_SIZE), 1)
        qtok = lax.broadcasted_iota(jnp.int32, (hq, PAGE_SIZE), 0) % tq
        online_update([jnp.where(key <= qtok, s, NEG_INF)], [vn_ref[...]])
        o = acc_ref[...] / l_ref[...]
        lane = lax.broadcasted_iota(jnp.int32, (tq, LANES), 1)
        tiles = []
        for t in range(N_HEADS // 2):
            a = o[(2 * t) * tq:(2 * t + 1) * tq, t * LANES:(t + 1) * LANES]
            b = o[(2 * t + 1) * tq:(2 * t + 2) * tq, t * LANES:(t + 1) * LANES]
            tiles.append(jnp.where(lane < HEAD_DIM, a, b))
        o_ref[...] = jnp.concatenate(tiles, axis=1)


def _attn_sample(page_table, q, kt_new, vt_new, lft_new, cache_kt, cache_vt, cache_lft, layer):
    Bd, tq, _ = q.shape
    n_pages = page_table.shape[1]
    pg = _pick_tile(n_pages, 16)
    ng = n_pages // pg
    pt_flat = page_table.reshape(-1)

    def page_spec(rows, i):
        return pl.BlockSpec((None, None, rows, PAGE_SIZE),
                            lambda b, j, pt: (layer, pt[b * n_pages + j * pg + i], 0, 0))

    seq_spec = lambda shape: pl.BlockSpec((None,) + shape, lambda b, j, pt: (b, 0, 0))
    in_specs = ([seq_spec((tq, D_ATTN)), seq_spec((D_ATTN, PAGE_SIZE)), seq_spec((D_ATTN, PAGE_SIZE)),
                 seq_spec((N_HEADS, PAGE_SIZE))]
                + [page_spec(D_ATTN, i) for i in range(pg)] + [page_spec(D_ATTN, i) for i in range(pg)]
                + [page_spec(N_HEADS, i) for i in range(pg)])
    hq = N_HEADS * tq
    grid_spec = pltpu.PrefetchScalarGridSpec(
        num_scalar_prefetch=1, grid=(Bd, ng), in_specs=in_specs,
        out_specs=pl.BlockSpec((None, tq, D_ATTN), lambda b, j, pt: (b, 0, 0)),
        scratch_shapes=[pltpu.VMEM((hq, D_ATTN), BF16), pltpu.VMEM((hq, D_ATTN), F32),
                        pltpu.VMEM((hq, 1), F32), pltpu.VMEM((hq, 1), F32), pltpu.VMEM((N_HEADS, PAGE_SIZE), F32)])
    return pl.pallas_call(
        functools.partial(_attn_sample_kernel, pg=pg, tq=tq),
        grid_spec=grid_spec,
        out_shape=jax.ShapeDtypeStruct((Bd, tq, D_ATTN), F32),
        compiler_params=_params(("arbitrary", "arbitrary")),
        name="attn_sample",
    )(pt_flat, q, kt_new, vt_new, lft_new, *([cache_kt] * pg), *([cache_vt] * pg), *([cache_lft] * pg))


def _s5_discretize(p):
    lr, li = p["lam_re"], p["lam_im"]
    dt = jnp.exp(p["log_dt"])[:, None]
    mag = jnp.exp(lr * dt)
    ang = li * dt
    ab_re = mag * jnp.cos(ang)
    ab_im = mag * jnp.sin(ang)
    den = lr * lr + li * li
    n_re = ab_re - 1.0
    n_im = ab_im
    f_re = (n_re * lr + n_im * li) / den
    f_im = (n_im * lr - n_re * li) / den
    bb_re = f_re[..., None] * p["b_re"] - f_im[..., None] * p["b_im"]
    bb_im = f_re[..., None] * p["b_im"] + f_im[..., None] * p["b_re"]
    return ab_re, ab_im, bb_re, bb_im


def _s5_block_weights(p):
    R = S5_BLOCK
    ab_re, ab_im, bb_re, bb_im = _s5_discretize(p)
    G, P, C = bb_re.shape
    pw_re, pw_im = [jnp.ones_like(ab_re)], [jnp.zeros_like(ab_im)]
    for _ in range(R):
        pr, pi = pw_re[-1], pw_im[-1]
        pw_re.append(pr * ab_re - pi * ab_im)
        pw_im.append(pr * ab_im + pi * ab_re)
    pw_re = jnp.stack(pw_re)
    pw_im = jnp.stack(pw_im)
    c_re, c_im = p["c_re"], p["c_im"]
    cp_re = c_re[None] * pw_re[:, :, None, :] - c_im[None] * pw_im[:, :, None, :]
    cp_im = c_re[None] * pw_im[:, :, None, :] + c_im[None] * pw_re[:, :, None, :]
    m = (jnp.einsum('jgcp,gpd->jgdc', cp_re[:R], bb_re, precision=HI)
         - jnp.einsum('jgcp,gpd->jgdc', cp_im[:R], bb_im, precision=HI))
    lag = jnp.arange(R)[None, :] - jnp.arange(R)[:, None]
    kt = jnp.where((lag >= 0)[:, :, None, None, None], m[jnp.clip(lag, 0, R - 1)], 0.0)
    kt = kt.transpose(2, 0, 3, 1, 4).reshape(G, R * C, R * C)
    pb_re = pw_re[R - 1::-1][:R]
    pb_im = pw_im[R - 1::-1][:R]
    x_re = pb_re[..., None] * bb_re[None] - pb_im[..., None] * bb_im[None]
    x_im = pb_re[..., None] * bb_im[None] + pb_im[..., None] * bb_re[None]
    x_re = x_re.transpose(1, 0, 3, 2).reshape(G, R * C, P)
    x_im = x_im.transpose(1, 0, 3, 2).reshape(G, R * C, P)
    wx = jnp.concatenate([x_re, x_im, x_im, x_re], axis=-1)
    wc_re = cp_re[1:].transpose(1, 3, 0, 2).reshape(G, P, R * C)
    wc_im = cp_im[1:].transpose(1, 3, 0, 2).reshape(G, P, R * C)
    wc = jnp.concatenate([wc_re, -wc_im], axis=1)
    a_blk = jnp.concatenate([pw_re[R], pw_re[R]], axis=-1)
    b_blk = jnp.concatenate([-pw_im[R], pw_im[R]], axis=-1)
    d = jnp.tile(p["d_skip"].reshape(G, 1, C), (1, 1, R))
    return kt.astype(BF16), wx.astype(BF16), wc.astype(BF16), a_blk, b_blk, d


def _s5_prompt_kernel(u_ref, kt_ref, wx_ref, wc_ref, a_ref, b_ref, d_ref, y_ref, hfin_ref,
                      ug_ref, xa_ref, xb_ref, hp_ref, h_ref, hsw_ref, *, tb, ng):
    R, C = S5_BLOCK, SSM_GROUP
    gpt = LANES // C
    sp = 2 * STATE_P

    @pl.when(pl.program_id(1) == 0)
    def _():
        h_ref[...] = jnp.zeros_like(h_ref)
        hsw_ref[...] = jnp.zeros_like(hsw_ref)

    lane_chunk = lax.broadcasted_iota(jnp.int32, (tb, LANES), 1) // C

    def chunk_transpose(vs):
        vs = list(vs)
        step = gpt // 2
        while step >= 1:
            low = (lane_chunk // step) % 2 == 0
            for i in range(gpt):
                if (i // step) % 2 == 0:
                    a, b = vs[i], vs[i + step]
                    vs[i] = jnp.where(low, a, pltpu.roll(b, step * C, 1))
                    vs[i + step] = jnp.where(low, pltpu.roll(a, LANES - step * C, 1), b)
            step //= 2
        return vs

    for qd in range(ng // gpt):
        for half in range(R // gpt):
            toks = [u_ref[qd, pl.ds(half * gpt + r8, tb, stride=R), :] for r8 in range(gpt)]
            for gl, v in enumerate(chunk_transpose(toks)):
                ug_ref[qd * gpt + gl, :, half * LANES:(half + 1) * LANES] = v

    def xbody(g, carry):
        x = _dot(ug_ref[g].astype(BF16), wx_ref[g])
        rows = pl.ds(pl.multiple_of(g * tb, tb), tb)
        xa_ref[rows, :] = x[:, :sp]
        xb_ref[rows, :] = x[:, sp:]
        return carry

    lax.fori_loop(0, ng, xbody, 0)

    a = a_ref[...]
    b = b_ref[...]

    def sbody(k, carry):
        h, hsw = carry
        rows = pl.ds(k, ng, stride=tb)
        hp_ref[rows, :] = h
        return a * h + b * hsw + xa_ref[rows, :], a * hsw - b * h + xb_ref[rows, :]

    h, hsw = lax.fori_loop(0, tb, sbody, (h_ref[...], hsw_ref[...]))
    h_ref[...] = h
    hsw_ref[...] = hsw
    hfin_ref[...] = h

    def ybody(g, carry):
        u = ug_ref[g]
        hp = hp_ref[pl.ds(pl.multiple_of(g * tb, tb), tb), :].astype(BF16)
        ug_ref[g] = _dot(u.astype(BF16), kt_ref[g]) + _dot(hp, wc_ref[g]) + u * d_ref[g]
        return carry

    lax.fori_loop(0, ng, ybody, 0)

    for qd in range(ng // gpt):
        for half in range(R // gpt):
            grps = [ug_ref[qd * gpt + gl, :, half * LANES:(half + 1) * LANES] for gl in range(gpt)]
            for r8, v in enumerate(chunk_transpose(grps)):
                y_ref[qd, pl.ds(half * gpt + r8, tb, stride=R), :] = v


def _s5_prompt(u, weights):
    kt, wx, wc, a_blk, b_blk, d = weights
    B, nq, T, _ = u.shape
    G = kt.shape[0]
    R = S5_BLOCK
    tb = _pick_tile(T // R, 128)
    sp = 2 * STATE_P
    blk_spec = pl.BlockSpec((None, nq, tb * R, LANES), lambda b, t: (b, 0, t, 0))
    return pl.pallas_call(
        functools.partial(_s5_prompt_kernel, tb=tb, ng=G),
        grid=(B, T // (tb * R)),
        in_specs=[blk_spec] + [_const_spec(x.shape) for x in (kt, wx, wc, a_blk, b_blk, d)],
        out_specs=[blk_spec, pl.BlockSpec((None, G, sp), lambda b, t: (b, 0, 0))],
        out_shape=[jax.ShapeDtypeStruct(u.shape, F32), jax.ShapeDtypeStruct((B, G, sp), F32)],
        scratch_shapes=[pltpu.VMEM((G, tb, R * SSM_GROUP), F32),
                        pltpu.VMEM((G * tb, sp), F32), pltpu.VMEM((G * tb, sp), F32), pltpu.VMEM((G * tb, sp), F32),
                        pltpu.VMEM((G, sp), F32), pltpu.VMEM((G, sp), F32)],
        compiler_params=_params(("arbitrary", "arbitrary")),
        name="s5_prompt",
    )(u, kt, wx, wc, a_blk, b_blk, d)


def _s5_sample_kernel(u_ref, h0_ref, bbd_ref, cmat_ref, are_ref, aim_ref, d_ref, y_ref, ht_ref, hs_ref, *, nt, nb):
    ns = are_ref.shape[1]
    u = u_ref[...]
    ub = u.astype(BF16)
    cw = 256
    for c0 in range(0, ns, cw):
        x_re = _dot(ub, bbd_ref[:, c0:c0 + cw])
        x_im = _dot(ub, bbd_ref[:, ns + c0:ns + c0 + cw])
        are = are_ref[:, c0:c0 + cw]
        aim = aim_ref[:, c0:c0 + cw]
        hr = h0_ref[:, c0:c0 + cw]
        hi = h0_ref[:, ns + c0:ns + c0 + cw]
        for t in range(nt):
            rows = slice(t * nb, (t + 1) * nb)
            hr, hi = are * hr - aim * hi + x_re[rows], are * hi + aim * hr + x_im[rows]
            hs_ref[rows, c0:c0 + cw] = hr.astype(BF16)
            hs_ref[rows, ns + c0:ns + c0 + cw] = hi.astype(BF16)
        ht_ref[:, c0:c0 + cw] = hr
        ht_ref[:, ns + c0:ns + c0 + cw] = hi
    y_ref[...] = _dot(hs_ref[...], cmat_ref[...]) + u * d_ref[...]


def _s5_sample(u_tm, h0, p, nt):
    ab_re, ab_im, bb_re, bb_im = _s5_discretize(p)
    G, P, C = bb_re.shape
    eye = jnp.eye(G, dtype=F32)
    bd = lambda w: jnp.einsum('gpc,gk->gckp', w, eye).reshape(G * C, G * P)
    bbd = jnp.concatenate([bd(bb_re), bd(bb_im)], axis=1).astype(BF16)
    cd = lambda w: jnp.einsum('gcp,gk->gpkc', w, eye).reshape(G * P, G * C)
    cmat = jnp.concatenate([cd(p["c_re"]), -cd(p["c_im"])], axis=0).astype(BF16)
    nb = h0.shape[0]
    ins = [u_tm, h0, bbd, cmat, ab_re.reshape(1, G * P), ab_im.reshape(1, G * P), p["d_skip"].reshape(1, G * C)]
    return pl.pallas_call(
        functools.partial(_s5_sample_kernel, nt=nt, nb=nb),
        grid=(1,),
        in_specs=[_const_spec(x.shape) for x in ins],
        out_specs=[pl.BlockSpec(u_tm.shape, lambda i: (0, 0)), pl.BlockSpec(h0.shape, lambda i: (0, 0))],
        out_shape=[jax.ShapeDtypeStruct(u_tm.shape, F32), jax.ShapeDtypeStruct(h0.shape, F32)],
        scratch_shapes=[pltpu.VMEM((nt * nb, 2 * G * P), BF16)],
        compiler_params=_params(("arbitrary",)),
        name="s5_sample",
    )(*ins)


def _mix_ffn_kernel(x_ref, s_ref, attn_ref, sga_ref, sgs_ref, prev_ref,
                    wa_ref, wb_ref, wao_ref, wout_ref, gffn_ref, wg_ref, wup_ref, wconv_ref, bconv_ref, wd_ref,
                    gfin_ref, y_ref, conv_ref, gate_ref, *, tm, shift, pad):
    hist = (CONV_W - 1) * shift

    @pl.when(pl.program_id(1) == 0)
    def _():
        gate_ref[pad - hist:pad, :] = prev_ref[...]

    s = jnp.concatenate([s_ref[qd] for qd in range(s_ref.shape[0])], axis=1)
    zb = jax.nn.gelu(s).astype(BF16)
    ssm_out = _dot(zb, wa_ref[...]) * jax.nn.sigmoid(_dot(zb, wb_ref[...]))
    attn_out = _dot(attn_ref[...], wao_ref[...])
    merged = sgs_ref[...] * ssm_out + sga_ref[...] * attn_out
    x1 = x_ref[...] + _dot(merged.astype(BF16), wout_ref[...])
    h2 = _rmsnorm(x1, gffn_ref[...]).astype(BF16)
    gate_ref[pad:pad + tm, :] = _dot(h2, wg_ref[...])
    gc = gate_ref[pad - hist:pad - hist + tm, :] * wconv_ref[0:1, :]
    for i in range(1, CONV_W):
        gc = gc + gate_ref[pad - hist + i * shift:pad - hist + i * shift + tm, :] * wconv_ref[i:i + 1, :]
    gc = bconv_ref[...] + gc
    act = (jax.nn.gelu(gc) * _dot(h2, wup_ref[...])).astype(BF16)
    x2 = x1 + _dot(act, wd_ref[...])
    y_ref[...] = _rmsnorm(x2, gfin_ref[...])
    tail = gate_ref[pad + tm - hist:pad + tm, :]
    conv_ref[...] = tail
    gate_ref[pad - hist:pad, :] = tail


def _mix_ffn(x, s, attn, sga, sgs, prev, w, *, shift, tm_pref):
    B, T, D = x.shape
    tm = _pick_tile(T, tm_pref)
    hist = (CONV_W - 1) * shift
    pad = -(-hist // 8) * 8
    F = w["wg"].shape[1]
    row_spec = lambda n: pl.BlockSpec((None, tm, n), lambda b, t: (b, t, 0))
    seq_spec = pl.BlockSpec((None, hist, F), lambda b, t: (b, 0, 0))
    consts = [w["wa"], w["wb"], w["wao"], w["wout"], w["gffn"], w["wg"], w["wup"], w["wconv"], w["bconv"], w["wd"],
              w["gfin"]]
    return pl.pallas_call(
        functools.partial(_mix_ffn_kernel, tm=tm, shift=shift, pad=pad),
        grid=(B, T // tm),
        in_specs=[row_spec(D), pl.BlockSpec((None, s.shape[1], tm, LANES), lambda b, t: (b, 0, t, 0)),
                  row_spec(attn.shape[2]), row_spec(D), row_spec(D), seq_spec]
        + [_const_spec(a.shape) for a in consts],
        out_specs=[row_spec(D), seq_spec],
        out_shape=[jax.ShapeDtypeStruct((B, T, D), F32), jax.ShapeDtypeStruct((B, hist, F), F32)],
        scratch_shapes=[pltpu.VMEM((pad + tm, F), F32)],
        compiler_params=_params(("arbitrary", "arbitrary")),
        name="mix_ffn_s%d" % shift,
    )(x, s, attn, sga, sgs, prev, *consts)


def _pack_weights(l, g_mix, w_in, b_f, w_glu_a, w_glu_b, w_attn_out, w_out, g_ffn, w_gate, w_up, w_conv, b_conv,
                  w_down, g_final):
    D = w_in.shape[1]
    d_ssm = w_glu_a.shape[1]
    o = [0, D_ATTN, 2 * D_ATTN, 3 * D_ATTN, 3 * D_ATTN + N_HEADS, 3 * D_ATTN + N_HEADS + d_ssm]
    wi = w_in[l]
    wq = wi[:, o[0]:o[1]] * SCALE
    w = {
        "wqkv": jnp.concatenate([wq, wi[:, o[1]:o[3]]], axis=1).astype(BF16),
        "wf": jnp.pad(wi[:, o[3]:o[4]], ((0, 0), (0, LANES - N_HEADS))).astype(BF16),
        "bf": jnp.pad(b_f[l], (0, LANES - N_HEADS)).reshape(1, LANES),
        "wu": wi[:, o[4]:o[5]].astype(BF16),
        "wga": wi[:, o[5]:o[5] + D].astype(BF16),
        "wgs": wi[:, o[5] + D:o[5] + 2 * D].astype(BF16),
        "wa": w_glu_a[l].astype(BF16), "wb": w_glu_b[l].astype(BF16), "wao": w_attn_out[l].astype(BF16),
        "wout": w_out[l].astype(BF16), "gffn": g_ffn[l].reshape(1, D), "wg": w_gate[l].astype(BF16),
        "wup": w_up[l].astype(BF16), "wconv": w_conv[l], "bconv": b_conv[l].reshape(1, -1),
        "wd": w_down[l].astype(BF16), "gfin": g_final.reshape(1, D),
    }
    w["wqt"] = wi[:, o[0]:o[1]].T.astype(BF16)
    w["wvt"] = wi[:, o[2]:o[3]].T.astype(BF16)
    e = jnp.arange(3 * LANES)
    part, head = e // LANES, e % LANES
    colk = head * LANES + HEAD_DIM + part
    cols = jnp.arange(N_HEADS * LANES)[None, :]
    w["selk"] = jnp.where((head < N_HEADS)[:, None] & (cols == colk[:, None]), -1.0, 0.0).astype(BF16)
    return w


def kernel(x_prompt, x_sample, cache_k, cache_v, cache_logf, page_table, state_ssm_re, state_ssm_im, state_conv,
           g_mix, w_in, b_f, lam_re, lam_im, log_dt, b_re, b_im, c_re, c_im, d_skip, w_glu_a, w_glu_b,
           w_attn_out, w_out, g_ffn, w_gate, w_up, w_conv, b_conv, w_down, g_final):
    depth = w_in.shape[0]
    assert depth == 1, "final norm is fused into the layer kernel; one layer supported"
    l = 0
    Bp, T, D = x_prompt.shape
    Bd, Td, _ = x_sample.shape
    n_pool = cache_k.shape[1]
    G, P = lam_re.shape[1], lam_re.shape[2]
    C = SSM_GROUP
    F = w_gate.shape[2]
    R = S5_BLOCK
    assert T % R == 0

    w = _pack_weights(l, g_mix, w_in, b_f, w_glu_a, w_glu_b, w_attn_out, w_out, g_ffn, w_gate, w_up, w_conv,
                      b_conv, w_down, g_final)
    gm = g_mix[l].reshape(1, D)
    sp = {"lam_re": lam_re[l], "lam_im": lam_im[l], "log_dt": log_dt[l], "b_re": b_re[l], "b_im": b_im[l],
          "c_re": c_re[l], "c_im": c_im[l], "d_skip": d_skip[l]}

    k_p, v_p, lf_p, u_p, sga_p, sgs_p, qat, ka, vat = _in_proj(x_prompt, gm, w, prompt=True,
                                                                 attn_blk=_pick_tile(T, 512))
    attn_p = _attn_prompt(qat, ka, vat)
    s_p, hfin = _s5_prompt(u_p, _s5_block_weights(sp))
    zeros_c = jnp.zeros((Bp, CONV_W - 1, F), F32)
    y_p, conv_p = _mix_ffn(x_prompt, s_p, attn_p, sga_p, sgs_p, zeros_c, w, shift=1, tm_pref=256)

    xs_tm = x_sample.transpose(1, 0, 2).reshape(1, Td * Bd, D)
    q_s, k_s, v_s, lf_s, u_s, sga_s, sgs_s = _in_proj(xs_tm, gm, w, prompt=False)
    to_bt = lambda a: a.reshape(Td, Bd, a.shape[-1]).transpose(1, 0, 2)
    q_b, k_b, v_b, lf_b = to_bt(q_s), to_bt(k_s), to_bt(v_s), to_bt(lf_s)
    tpad = lambda a: jnp.pad(a.transpose(0, 2, 1), ((0, 0), (0, 0), (0, PAGE_SIZE - Td)))
    page_t = lambda c: c.transpose(0, 1, 3, 4, 2).reshape(depth, n_pool, D_ATTN, PAGE_SIZE)
    attn_s = _attn_sample(page_table, q_b, tpad(k_b), tpad(v_b), tpad(lf_b), page_t(cache_k), page_t(cache_v),
                          cache_logf.transpose(0, 1, 3, 2), l)
    h0 = jnp.concatenate([state_ssm_re[l].reshape(Bd, G * P), state_ssm_im[l].reshape(Bd, G * P)], axis=1)
    tiles = lambda a: a.reshape(a.shape[0], -1, LANES).transpose(1, 0, 2)
    s_s, ht = _s5_sample(u_s[0].transpose(1, 0, 2).reshape(Td * Bd, G * C), h0, sp, Td)
    attn_s_tm = attn_s.transpose(1, 0, 2).reshape(1, Td * Bd, D_ATTN).astype(BF16)
    prev_s = state_conv[l].transpose(1, 0, 2).reshape(1, (CONV_W - 1) * Bd, F)
    y_s, conv_s = _mix_ffn(xs_tm, tiles(s_s)[None], attn_s_tm, sga_s, sgs_s, prev_s, w,
                           shift=Bd, tm_pref=256)

    y_sample = y_s.reshape(Td, Bd, D).transpose(1, 0, 2)
    conv_sample = conv_s.reshape(CONV_W - 1, Bd, F).transpose(1, 0, 2)
    hd = lambda a, b, t: a.reshape(1, b, t, N_HEADS, HEAD_DIM)
    return (y_p, y_sample,
            hd(k_p, Bp, T), v_p.reshape(Bp, N_HEADS, HEAD_DIM, T).transpose(0, 3, 1, 2)[None],
            lf_p.reshape(1, Bp, T, N_HEADS),
            hd(k_b, Bd, Td), hd(v_b, Bd, Td), lf_b.reshape(1, Bd, Td, N_HEADS),
            hfin[:, :, :P].reshape(1, Bp, G, P), hfin[:, :, P:].reshape(1, Bp, G, P),
            ht[:, :G * P].reshape(1, Bd, G, P), ht[:, G * P:].reshape(1, Bd, G, P),
            conv_p.reshape(1, Bp, CONV_W - 1, F), conv_sample.reshape(1, Bd, CONV_W - 1, F))
```

```python
import functools

import jax
import jax.numpy as jnp
from jax import lax
from jax.experimental import pallas as pl
from jax.experimental.pallas import tpu as pltpu

N_HEADS = 8
HEAD_DIM = 64
D_ATTN = N_HEADS * HEAD_DIM
SSM_GROUP = 16
STATE_P = 64
CONV_W = 3
PAGE_SIZE = 128
RMS_EPS = 1e-6
NEG_INF = -1e30
SCALE = HEAD_DIM ** -0.5
LOG2E = 1.4426950408889634

LANES = 128
VT_ROWS = 80
S5_BLOCK = 16
VMEM_LIMIT = 56 * 1024 * 1024

F32 = jnp.float32
BF16 = jnp.bfloat16
HI = lax.Precision.HIGHEST


def _dot(a, b):
    return jnp.dot(a, b, preferred_element_type=F32)


def _dot_nt(a, b):
    return lax.dot_general(a, b, (((1,), (1,)), ((), ())), preferred_element_type=F32)


def _split3(x):
    hi = x.astype(BF16)
    r1 = x - hi.astype(F32)
    mid = r1.astype(BF16)
    lo = (r1 - mid.astype(F32)).astype(BF16)
    return hi, mid, lo


def _rmsnorm(x, g):
    return x * lax.rsqrt(jnp.mean(x * x, axis=-1, keepdims=True) + RMS_EPS) * g


def _pick_tile(n, pref):
    t = min(n, pref)
    while n % t:
        t //= 2
    return t


def _const_spec(shape):
    nd = len(shape)
    return pl.BlockSpec(shape, lambda *_: (0,) * nd, pipeline_mode=pl.Buffered(1))


def _params(sem):
    return pltpu.CompilerParams(dimension_semantics=sem, vmem_limit_bytes=VMEM_LIMIT)


def _in_proj_kernel(*refs, prompt, tm):
    if prompt:
        (x_ref, g_ref, wqkv_ref, wf_ref, bf_ref, wu_ref, wga_ref, wgs_ref, wqt_ref, wvt_ref, selk_ref,
         k_ref, v_ref, lf_ref, u_ref, sga_ref, sgs_ref, qat_ref, ka_ref, vat_ref, carry_ref) = refs
    else:
        (x_ref, g_ref, wqkv_ref, wf_ref, bf_ref, wu_ref, wga_ref, wgs_ref,
         q_ref, k_ref, v_ref, lf_ref, u_ref, sga_ref, sgs_ref) = refs

    hb = _rmsnorm(x_ref[...], g_ref[...]).astype(BF16)
    logf = jax.nn.log_sigmoid(_dot(hb, wf_ref[...]) + bf_ref[...])
    lf_ref[...] = logf[:, :N_HEADS]
    u = _dot(hb, wu_ref[...])
    for qd in range(u.shape[1] // LANES):
        u_ref[qd] = u[:, qd * LANES:(qd + 1) * LANES]
    sga_ref[...] = jax.nn.sigmoid(_dot(hb, wga_ref[...]))
    sgs_ref[...] = jax.nn.sigmoid(_dot(hb, wgs_ref[...]))
    if not prompt:
        qkv = _dot(hb, wqkv_ref[...])
        q_ref[...] = qkv[:, :D_ATTN]
        k_ref[...] = qkv[:, D_ATTN:2 * D_ATTN]
        v_ref[...] = qkv[:, 2 * D_ATTN:]
        return

    k = _dot(hb, wqkv_ref[:, D_ATTN:2 * D_ATTN])
    k_ref[...] = k

    @pl.when(pl.program_id(1) == 0)
    def _():
        carry_ref[...] = jnp.zeros_like(carry_ref)

    row = lax.broadcasted_iota(jnp.int32, (tm, tm), 0)
    col = lax.broadcasted_iota(jnp.int32, (tm, tm), 1)
    tri = jnp.where(row >= col, 1.0, 0.0).astype(BF16)
    l_hi, l_mid, l_lo = _split3(logf)
    c = _dot(tri, l_hi) + _dot(tri, l_mid) + _dot(tri, l_lo) + carry_ref[...]
    carry_ref[...] = c[tm - 1:tm, :]

    e = jnp.concatenate(_split3(c * LOG2E), axis=1)
    ek = _dot(e, selk_ref[...])
    lane = lax.broadcasted_iota(jnp.int32, (tm, LANES), 1)
    low = lane < HEAD_DIM
    for h in range(N_HEADS):
        t = h // 2
        kt = k[:, t * LANES:(t + 1) * LANES]
        if h % 2:
            kt = pltpu.roll(kt, HEAD_DIM, 1)
        ka_ref[h] = jnp.where(low, kt, ek[:, h * LANES:(h + 1) * LANES]).astype(BF16)
    qt = _dot_nt(wqt_ref[...], hb) * (SCALE * LOG2E)
    vt = _dot_nt(wvt_ref[...], hb)
    v_ref[...] = vt
    sub = lax.broadcasted_iota(jnp.int32, (HEAD_DIM, tm), 0)
    q_ones = jnp.where(sub < 3, 1.0, 0.0)
    sub_v = lax.broadcasted_iota(jnp.int32, (VT_ROWS - HEAD_DIM, tm), 0)
    v_ones = jnp.where(sub_v == 0, 1.0, 0.0)
    for h in range(N_HEADS):
        rows = slice(h * HEAD_DIM, (h + 1) * HEAD_DIM)
        qat_ref[h] = jnp.concatenate([qt[rows], q_ones], axis=0).astype(BF16)
        vat_ref[h] = jnp.concatenate([vt[rows], v_ones], axis=0).astype(BF16)


def _in_proj(x, g_mix, w, *, prompt, attn_blk=None):
    B, T, D = x.shape
    tm = _pick_tile(T, 256)
    nt = T // tm
    d_gate = w["wga"].shape[1]
    d_ssm = w["wu"].shape[1]
    row_spec = lambda n: pl.BlockSpec((None, tm, n), lambda b, t: (b, t, 0))
    ins = [x, g_mix, w["wqkv"], w["wf"], w["bf"], w["wu"], w["wga"], w["wgs"]]
    in_specs = [row_spec(D)] + [_const_spec(a.shape) for a in ins[1:]]
    outs = [jax.ShapeDtypeStruct((B, T, D_ATTN), F32),
            jax.ShapeDtypeStruct((B, T, D_ATTN), F32),
            jax.ShapeDtypeStruct((B, T, N_HEADS), F32),
            jax.ShapeDtypeStruct((B, d_ssm // LANES, T, LANES), F32),
            jax.ShapeDtypeStruct((B, T, d_gate), F32),
            jax.ShapeDtypeStruct((B, T, d_gate), F32)]
    out_specs = [row_spec(D_ATTN), row_spec(D_ATTN), row_spec(N_HEADS),
                 pl.BlockSpec((None, d_ssm // LANES, tm, LANES), lambda b, t: (b, 0, t, 0)),
                 row_spec(d_gate), row_spec(d_gate)]
    scratch = []
    if prompt:
        outs[1] = jax.ShapeDtypeStruct((B, D_ATTN, T), F32)
        out_specs[1] = pl.BlockSpec((None, D_ATTN, tm), lambda b, t: (b, 0, t))
        extra = [w["wqt"], w["wvt"], w["selk"]]
        ins += extra
        in_specs += [_const_spec(a.shape) for a in extra]
        per = attn_blk // tm
        outs += [jax.ShapeDtypeStruct((B, N_HEADS, LANES, T), BF16),
                 jax.ShapeDtypeStruct((B, N_HEADS, T, LANES), BF16),
                 jax.ShapeDtypeStruct((B, N_HEADS, T // attn_blk, VT_ROWS, attn_blk), BF16)]
        out_specs += [pl.BlockSpec((None, N_HEADS, LANES, tm), lambda b, t: (b, 0, 0, t)),
                      pl.BlockSpec((None, N_HEADS, tm, LANES), lambda b, t: (b, 0, t, 0)),
                      pl.BlockSpec((None, N_HEADS, None, VT_ROWS, tm), lambda b, t: (b, 0, t // per, 0, t % per))]
        scratch = [pltpu.VMEM((1, LANES), F32)]
    else:
        outs = [jax.ShapeDtypeStruct((B, T, D_ATTN), F32)] + outs
        out_specs = [row_spec(D_ATTN)] + out_specs
    return pl.pallas_call(
        functools.partial(_in_proj_kernel, prompt=prompt, tm=tm),
        grid=(B, nt), in_specs=in_specs, out_specs=out_specs, out_shape=outs, scratch_shapes=scratch,
        compiler_params=_params(("arbitrary", "arbitrary")),
        name="in_proj_prompt" if prompt else "in_proj_sample",
    )(*ins)


def _attn_prompt_kernel(qt_ref, k_ref, vt_ref, o_ref, acc0_ref, acc1_ref, *, bq, bk):
    qi = pl.program_id(2)
    ratio = bq // bk
    accs = (acc0_ref, acc1_ref)
    for a in accs:
        a[...] = jnp.zeros_like(a)

    def update(j, ms, diag):
        start = pl.multiple_of(j * bk, bk)
        q0 = 0 if diag is None else diag * bk
        new_ms = []
        for hh in range(2):
            st = _dot(k_ref[hh, pl.ds(start, bk), :], qt_ref[hh, :, q0:])
            if diag is not None:
                key = lax.broadcasted_iota(jnp.int32, st.shape, 0)
                qry = lax.broadcasted_iota(jnp.int32, st.shape, 1)
                st = jnp.where(key <= qry, st, NEG_INF)
            m_old = ms[hh][:, q0:]
            m_new = jnp.maximum(m_old, jnp.max(st, axis=0, keepdims=True))
            p = jnp.exp2(st - m_new).astype(BF16)
            accs[hh][:, q0:] = jnp.exp2(m_old - m_new) * accs[hh][:, q0:] + _dot(vt_ref[hh, j], p)
            new_ms.append(m_new if q0 == 0 else jnp.concatenate([ms[hh][:, :q0], m_new], axis=1))
        return tuple(new_ms)

    m0 = jnp.full((1, bq), NEG_INF, F32)
    ms = lax.fori_loop(0, qi * ratio, lambda j, ms: update(j, ms, None), (m0, m0))
    for d in range(ratio):
        ms = update(qi * ratio + d, ms, d)
    outs = []
    for a in accs:
        acc = a[...]
        outs.append(acc[:HEAD_DIM] / acc[HEAD_DIM:HEAD_DIM + 1])
    o_ref[...] = jnp.concatenate(outs, axis=0).T.astype(o_ref.dtype)


def _attn_prompt(qat, ka, vat):
    B, H, T, _ = ka.shape
    nk, bk = vat.shape[2], vat.shape[4]
    bq = _pick_tile(T, 4 * bk)
    return pl.pallas_call(
        functools.partial(_attn_prompt_kernel, bq=bq, bk=bk),
        grid=(B, H // 2, T // bq),
        in_specs=[pl.BlockSpec((None, 2, LANES, bq), lambda b, hp, qi: (b, hp, 0, qi)),
                  pl.BlockSpec((None, 2, T, LANES), lambda b, hp, qi: (b, hp, 0, 0)),
                  pl.BlockSpec((None, 2, nk, VT_ROWS, bk), lambda b, hp, qi: (b, hp, 0, 0, 0))],
        out_specs=pl.BlockSpec((None, bq, LANES), lambda b, hp, qi: (b, qi, hp)),
        out_shape=jax.ShapeDtypeStruct((B, T, D_ATTN), BF16),
        scratch_shapes=[pltpu.VMEM((VT_ROWS, bq), F32), pltpu.VMEM((VT_ROWS, bq), F32)],
        compiler_params=_params(("arbitrary", "arbitrary", "arbitrary")),
        name="attn_prompt",
    )(qat, ka, vat)


def _attn_sample_kernel(pt_ref, q_ref, kn_ref, vn_ref, lfn_ref, *rest, pg, tq):
    k_refs = rest[:pg]
    v_refs = rest[pg:2 * pg]
    lf_refs = rest[2 * pg:3 * pg]
    o_ref, qbd_ref, acc_ref, m_ref, l_ref, coff_ref = rest[3 * pg:]
    del pt_ref
    j = pl.program_id(1)
    hq = N_HEADS * tq

    @pl.when(j == 0)
    def _():
        q = q_ref[...] * LOG2E
        lane_head = lax.broadcasted_iota(jnp.int32, (tq, D_ATTN), 1) // HEAD_DIM
        qbd_ref[...] = jnp.concatenate(
            [jnp.where(lane_head == h, q, 0.0) for h in range(N_HEADS)], axis=0).astype(BF16)
        acc_ref[...] = jnp.zeros_like(acc_ref)
        l_ref[...] = jnp.zeros_like(l_ref)
        m_ref[...] = jnp.full_like(m_ref, NEG_INF)
        coff_ref[...] = jnp.zeros_like(coff_ref)

    r = lax.broadcasted_iota(jnp.int32, (PAGE_SIZE, PAGE_SIZE), 0)
    cc = lax.broadcasted_iota(jnp.int32, (PAGE_SIZE, PAGE_SIZE), 1)
    triu = jnp.where(r <= cc, 1.0, 0.0).astype(BF16)
    qbd = qbd_ref[...]

    def cum_logf(lf_list, coff):
        n = len(lf_list)
        lf = jnp.concatenate(lf_list, axis=0) if n > 1 else lf_list[0]
        a, b, c3 = _split3(lf)
        w = _dot(a, triu) + _dot(b, triu) + _dot(c3, triu)
        tot = jnp.broadcast_to(w[:, PAGE_SIZE - 1:PAGE_SIZE], w.shape)
        pages = []
        for i in range(n):
            pages.append(w[i * N_HEADS:(i + 1) * N_HEADS] + coff)
            coff = coff + tot[i * N_HEADS:(i + 1) * N_HEADS]
        return pages, coff

    def page_scores(kt_page, ck):
        s = _dot(qbd, kt_page.astype(BF16))
        bias = jnp.concatenate(
            [jnp.broadcast_to(ck[h:h + 1, :], (tq, PAGE_SIZE)) for h in range(N_HEADS)], axis=0)
        return s - bias * LOG2E

    def online_update(s_list, vt_list):
        s = jnp.concatenate(s_list, axis=1) if len(s_list) > 1 else s_list[0]
        m_prev = m_ref[...]
        m_new = jnp.maximum(m_prev, jnp.max(s, axis=1, keepdims=True))
        p = jnp.exp2(s - m_new)
        alpha = jnp.exp2(m_prev - m_new)
        pv = None
        for i, vt_page in enumerate(vt_list):
            d = _dot_nt(p[:, i * PAGE_SIZE:(i + 1) * PAGE_SIZE].astype(BF16), vt_page.astype(BF16))
            pv = d if pv is None else pv + d
        acc_ref[...] = alpha * acc_ref[...] + pv
        l_ref[...] = alpha * l_ref[...] + jnp.sum(p, axis=1, keepdims=True)
        m_ref[...] = m_new

    cks, coff = cum_logf([lf_refs[i][...] for i in range(pg)], coff_ref[...])
    coff_ref[...] = coff
    online_update([page_scores(k_refs[i][...], cks[i]) for i in range(pg)], [v_refs[i][...] for i in range(pg)])

    @pl.when(j == pl.num_programs(1) - 1)
    def _():
        cnew, _unused = cum_logf([lfn_ref[...]], coff_ref[...])
        s = page_scores(kn_ref[...], cnew[0])
        key = lax.broadcasted_iota(jnp.int32, (hq, PAGE_SIZE), 1)
        qtok = lax.broadcasted_iota(jnp.int32, (hq, PAGE_SIZE), 0) % tq
        online_update([jnp.where(key <= qtok, s, NEG_INF)], [vn_ref[...]])
        o = acc_ref[...] / l_ref[...]
        lane = lax.broadcasted_iota(jnp.int32, (tq, LANES), 1)
        tiles = []
        for t in range(N_HEADS // 2):
            a = o[(2 * t) * tq:(2 * t + 1) * tq, t * LANES:(t + 1) * LANES]
            b = o[(2 * t + 1) * tq:(2 * t + 2) * tq, t * LANES:(t + 1) * LANES]
            tiles.append(jnp.where(lane < HEAD_DIM, a, b))
        o_ref[...] = jnp.concatenate(tiles, axis=1)


def _attn_sample(page_table, q, kt_new, vt_new, lft_new, cache_kt, cache_vt, cache_lft, layer):
    Bd, tq, _ = q.shape
    n_pages = page_table.shape[1]
    pg = _pick_tile(n_pages, 32)
    ng = n_pages // pg
    pt_flat = page_table.reshape(-1)

    def page_spec(rows, i):
        return pl.BlockSpec((None, None, rows, PAGE_SIZE),
                            lambda b, j, pt: (layer, pt[b * n_pages + j * pg + i], 0, 0))

    seq_spec = lambda shape: pl.BlockSpec((None,) + shape, lambda b, j, pt: (b, 0, 0))
    in_specs = ([seq_spec((tq, D_ATTN)), seq_spec((D_ATTN, PAGE_SIZE)), seq_spec((D_ATTN, PAGE_SIZE)),
                 seq_spec((N_HEADS, PAGE_SIZE))]
                + [page_spec(D_ATTN, i) for i in range(pg)] + [page_spec(D_ATTN, i) for i in range(pg)]
                + [page_spec(N_HEADS, i) for i in range(pg)])
    hq = N_HEADS * tq
    grid_spec = pltpu.PrefetchScalarGridSpec(
        num_scalar_prefetch=1, grid=(Bd, ng), in_specs=in_specs,
        out_specs=pl.BlockSpec((None, tq, D_ATTN), lambda b, j, pt: (b, 0, 0)),
        scratch_shapes=[pltpu.VMEM((hq, D_ATTN), BF16), pltpu.VMEM((hq, D_ATTN), F32),
                        pltpu.VMEM((hq, 1), F32), pltpu.VMEM((hq, 1), F32), pltpu.VMEM((N_HEADS, PAGE_SIZE), F32)])
    return pl.pallas_call(
        functools.partial(_attn_sample_kernel, pg=pg, tq=tq),
        grid_spec=grid_spec,
        out_shape=jax.ShapeDtypeStruct((Bd, tq, D_ATTN), F32),
        compiler_params=_params(("arbitrary", "arbitrary")),
        name="attn_sample",
    )(pt_flat, q, kt_new, vt_new, lft_new, *([cache_kt] * pg), *([cache_vt] * pg), *([cache_lft] * pg))


def _s5_discretize(p):
    lr, li = p["lam_re"], p["lam_im"]
    dt = jnp.exp(p["log_dt"])[:, None]
    mag = jnp.exp(lr * dt)
    ang = li * dt
    ab_re = mag * jnp.cos(ang)
    ab_im = mag * jnp.sin(ang)
    den = lr * lr + li * li
    n_re = ab_re - 1.0
    n_im = ab_im
    f_re = (n_re * lr + n_im * li) / den
    f_im = (n_im * lr - n_re * li) / den
    bb_re = f_re[..., None] * p["b_re"] - f_im[..., None] * p["b_im"]
    bb_im = f_re[..., None] * p["b_im"] + f_im[..., None] * p["b_re"]
    return ab_re, ab_im, bb_re, bb_im


def _s5_block_weights(p):
    R = S5_BLOCK
    ab_re, ab_im, bb_re, bb_im = _s5_discretize(p)
    G, P, C = bb_re.shape
    pw_re, pw_im = [jnp.ones_like(ab_re)], [jnp.zeros_like(ab_im)]
    for _ in range(R):
        pr, pi = pw_re[-1], pw_im[-1]
        pw_re.append(pr * ab_re - pi * ab_im)
        pw_im.append(pr * ab_im + pi * ab_re)
    pw_re = jnp.stack(pw_re)
    pw_im = jnp.stack(pw_im)
    c_re, c_im = p["c_re"], p["c_im"]
    cp_re = c_re[None] * pw_re[:, :, None, :] - c_im[None] * pw_im[:, :, None, :]
    cp_im = c_re[None] * pw_im[:, :, None, :] + c_im[None] * pw_re[:, :, None, :]
    m = (jnp.einsum('jgcp,gpd->jgdc', cp_re[:R], bb_re, precision=HI)
         - jnp.einsum('jgcp,gpd->jgdc', cp_im[:R], bb_im, precision=HI))
    lag = jnp.arange(R)[None, :] - jnp.arange(R)[:, None]
    kt = jnp.where((lag >= 0)[:, :, None, None, None], m[jnp.clip(lag, 0, R - 1)], 0.0)
    kt = kt.transpose(2, 0, 3, 1, 4).reshape(G, R * C, R * C)
    pb_re = pw_re[R - 1::-1][:R]
    pb_im = pw_im[R - 1::-1][:R]
    x_re = pb_re[..., None] * bb_re[None] - pb_im[..., None] * bb_im[None]
    x_im = pb_re[..., None] * bb_im[None] + pb_im[..., None] * bb_re[None]
    x_re = x_re.transpose(1, 0, 3, 2).reshape(G, R * C, P)
    x_im = x_im.transpose(1, 0, 3, 2).reshape(G, R * C, P)
    wx = jnp.concatenate([x_re, x_im, x_im, x_re], axis=-1)
    wc_re = cp_re[1:].transpose(1, 3, 0, 2).reshape(G, P, R * C)
    wc_im = cp_im[1:].transpose(1, 3, 0, 2).reshape(G, P, R * C)
    wc = jnp.concatenate([wc_re, -wc_im], axis=1)
    a_blk = jnp.concatenate([pw_re[R], pw_re[R]], axis=-1)
    b_blk = jnp.concatenate([-pw_im[R], pw_im[R]], axis=-1)
    d = jnp.tile(p["d_skip"].reshape(G, 1, C), (1, 1, R))
    return kt.astype(BF16), wx.astype(BF16), wc.astype(BF16), a_blk, b_blk, d


def _s5_prompt_kernel(u_ref, kt_ref, wx_ref, wc_ref, a_ref, b_ref, d_ref, y_ref, hfin_ref,
                      ug_ref, xa_ref, xb_ref, hp_ref, h_ref, hsw_ref, *, tb, ng):
    R, C = S5_BLOCK, SSM_GROUP
    gpt = LANES // C
    sp = 2 * STATE_P

    @pl.when(pl.program_id(1) == 0)
    def _():
        h_ref[...] = jnp.zeros_like(h_ref)
        hsw_ref[...] = jnp.zeros_like(hsw_ref)

    lane_chunk = lax.broadcasted_iota(jnp.int32, (tb, LANES), 1) // C

    def chunk_transpose(vs):
        vs = list(vs)
        step = gpt // 2
        while step >= 1:
            low = (lane_chunk // step) % 2 == 0
            for i in range(gpt):
                if (i // step) % 2 == 0:
                    a, b = vs[i], vs[i + step]
                    vs[i] = jnp.where(low, a, pltpu.roll(b, step * C, 1))
                    vs[i + step] = jnp.where(low, pltpu.roll(a, LANES - step * C, 1), b)
            step //= 2
        return vs

    for qd in range(ng // gpt):
        for half in range(R // gpt):
            toks = [u_ref[qd, pl.ds(half * gpt + r8, tb, stride=R), :] for r8 in range(gpt)]
            for gl, v in enumerate(chunk_transpose(toks)):
                ug_ref[qd * gpt + gl, :, half * LANES:(half + 1) * LANES] = v

    def xbody(g, carry):
        x = _dot(ug_ref[g].astype(BF16), wx_ref[g])
        rows = pl.ds(pl.multiple_of(g * tb, tb), tb)
        xa_ref[rows, :] = x[:, :sp]
        xb_ref[rows, :] = x[:, sp:]
        return carry

    lax.fori_loop(0, ng, xbody, 0, unroll=4)

    a = a_ref[...]
    b = b_ref[...]

    def sbody(k, carry):
        h, hsw = carry
        rows = pl.ds(k, ng, stride=tb)
        hp_ref[rows, :] = h
        return a * h + b * hsw + xa_ref[rows, :], a * hsw - b * h + xb_ref[rows, :]

    h, hsw = lax.fori_loop(0, tb, sbody, (h_ref[...], hsw_ref[...]), unroll=8)
    h_ref[...] = h
    hsw_ref[...] = hsw
    hfin_ref[...] = h

    def ybody(g, carry):
        u = ug_ref[g]
        hp = hp_ref[pl.ds(pl.multiple_of(g * tb, tb), tb), :].astype(BF16)
        ug_ref[g] = _dot(u.astype(BF16), kt_ref[g]) + _dot(hp, wc_ref[g]) + u * d_ref[g]
        return carry

    lax.fori_loop(0, ng, ybody, 0, unroll=4)

    for qd in range(ng // gpt):
        for half in range(R // gpt):
            grps = [ug_ref[qd * gpt + gl, :, half * LANES:(half + 1) * LANES] for gl in range(gpt)]
            for r8, v in enumerate(chunk_transpose(grps)):
                y_ref[qd, pl.ds(half * gpt + r8, tb, stride=R), :] = v


def _s5_prompt(u, weights):
    kt, wx, wc, a_blk, b_blk, d = weights
    B, nq, T, _ = u.shape
    G = kt.shape[0]
    R = S5_BLOCK
    tb = _pick_tile(T // R, 128)
    sp = 2 * STATE_P
    blk_spec = pl.BlockSpec((None, nq, tb * R, LANES), lambda b, t: (b, 0, t, 0))
    return pl.pallas_call(
        functools.partial(_s5_prompt_kernel, tb=tb, ng=G),
        grid=(B, T // (tb * R)),
        in_specs=[blk_spec] + [_const_spec(x.shape) for x in (kt, wx, wc, a_blk, b_blk, d)],
        out_specs=[blk_spec, pl.BlockSpec((None, G, sp), lambda b, t: (b, 0, 0))],
        out_shape=[jax.ShapeDtypeStruct(u.shape, F32), jax.ShapeDtypeStruct((B, G, sp), F32)],
        scratch_shapes=[pltpu.VMEM((G, tb, R * SSM_GROUP), F32),
                        pltpu.VMEM((G * tb, sp), F32), pltpu.VMEM((G * tb, sp), F32), pltpu.VMEM((G * tb, sp), F32),
                        pltpu.VMEM((G, sp), F32), pltpu.VMEM((G, sp), F32)],
        compiler_params=_params(("arbitrary", "arbitrary")),
        name="s5_prompt",
    )(u, kt, wx, wc, a_blk, b_blk, d)


def _s5_sample_kernel(u_ref, h0_ref, bbd_ref, cmat_ref, are_ref, aim_ref, d_ref, y_ref, ht_ref, hs_ref, *, nt, nb):
    ns = are_ref.shape[1]
    u = u_ref[...]
    ub = u.astype(BF16)
    cw = 256
    for c0 in range(0, ns, cw):
        x_re = _dot(ub, bbd_ref[:, c0:c0 + cw])
        x_im = _dot(ub, bbd_ref[:, ns + c0:ns + c0 + cw])
        are = are_ref[:, c0:c0 + cw]
        aim = aim_ref[:, c0:c0 + cw]
        hr = h0_ref[:, c0:c0 + cw]
        hi = h0_ref[:, ns + c0:ns + c0 + cw]
        for t in range(nt):
            rows = slice(t * nb, (t + 1) * nb)
            hr, hi = are * hr - aim * hi + x_re[rows], are * hi + aim * hr + x_im[rows]
            hs_ref[rows, c0:c0 + cw] = hr.astype(BF16)
            hs_ref[rows, ns + c0:ns + c0 + cw] = hi.astype(BF16)
        ht_ref[:, c0:c0 + cw] = hr
        ht_ref[:, ns + c0:ns + c0 + cw] = hi
    y_ref[...] = _dot(hs_ref[...], cmat_ref[...]) + u * d_ref[...]


def _s5_sample(u_tm, h0, p, nt):
    ab_re, ab_im, bb_re, bb_im = _s5_discretize(p)
    G, P, C = bb_re.shape
    eye = jnp.eye(G, dtype=F32)
    bd = lambda w: jnp.einsum('gpc,gk->gckp', w, eye).reshape(G * C, G * P)
    bbd = jnp.concatenate([bd(bb_re), bd(bb_im)], axis=1).astype(BF16)
    cd = lambda w: jnp.einsum('gcp,gk->gpkc', w, eye).reshape(G * P, G * C)
    cmat = jnp.concatenate([cd(p["c_re"]), -cd(p["c_im"])], axis=0).astype(BF16)
    nb = h0.shape[0]
    ins = [u_tm, h0, bbd, cmat, ab_re.reshape(1, G * P), ab_im.reshape(1, G * P), p["d_skip"].reshape(1, G * C)]
    return pl.pallas_call(
        functools.partial(_s5_sample_kernel, nt=nt, nb=nb),
        grid=(1,),
        in_specs=[_const_spec(x.shape) for x in ins],
        out_specs=[pl.BlockSpec(u_tm.shape, lambda i: (0, 0)), pl.BlockSpec(h0.shape, lambda i: (0, 0))],
        out_shape=[jax.ShapeDtypeStruct(u_tm.shape, F32), jax.ShapeDtypeStruct(h0.shape, F32)],
        scratch_shapes=[pltpu.VMEM((nt * nb, 2 * G * P), BF16)],
        compiler_params=_params(("arbitrary",)),
        name="s5_sample",
    )(*ins)


def _mix_ffn_kernel(x_ref, s_ref, attn_ref, sga_ref, sgs_ref, prev_ref,
                    wa_ref, wb_ref, wao_ref, wout_ref, gffn_ref, wg_ref, wup_ref, wconv_ref, bconv_ref, wd_ref,
                    gfin_ref, y_ref, conv_ref, gate_ref, *, tm, shift, pad):
    hist = (CONV_W - 1) * shift

    @pl.when(pl.program_id(1) == 0)
    def _():
        gate_ref[pad - hist:pad, :] = prev_ref[...]

    s = jnp.concatenate([s_ref[qd] for qd in range(s_ref.shape[0])], axis=1)
    zb = jax.nn.gelu(s).astype(BF16)
    ssm_out = _dot(zb, wa_ref[...]) * jax.nn.sigmoid(_dot(zb, wb_ref[...]))
    attn_out = _dot(attn_ref[...], wao_ref[...])
    merged = sgs_ref[...] * ssm_out + sga_ref[...] * attn_out
    x1 = x_ref[...] + _dot(merged.astype(BF16), wout_ref[...])
    h2 = _rmsnorm(x1, gffn_ref[...]).astype(BF16)
    gate_ref[pad:pad + tm, :] = _dot(h2, wg_ref[...])
    gc = gate_ref[pad - hist:pad - hist + tm, :] * wconv_ref[0:1, :]
    for i in range(1, CONV_W):
        gc = gc + gate_ref[pad - hist + i * shift:pad - hist + i * shift + tm, :] * wconv_ref[i:i + 1, :]
    gc = bconv_ref[...] + gc
    act = (jax.nn.gelu(gc) * _dot(h2, wup_ref[...])).astype(BF16)
    x2 = x1 + _dot(act, wd_ref[...])
    y_ref[...] = _rmsnorm(x2, gfin_ref[...])
    tail = gate_ref[pad + tm - hist:pad + tm, :]
    conv_ref[...] = tail
    gate_ref[pad - hist:pad, :] = tail


def _mix_ffn(x, s, attn, sga, sgs, prev, w, *, shift, tm_pref):
    B, T, D = x.shape
    tm = _pick_tile(T, tm_pref)
    hist = (CONV_W - 1) * shift
    pad = -(-hist // 8) * 8
    F = w["wg"].shape[1]
    row_spec = lambda n: pl.BlockSpec((None, tm, n), lambda b, t: (b, t, 0))
    seq_spec = pl.BlockSpec((None, hist, F), lambda b, t: (b, 0, 0))
    consts = [w["wa"], w["wb"], w["wao"], w["wout"], w["gffn"], w["wg"], w["wup"], w["wconv"], w["bconv"], w["wd"],
              w["gfin"]]
    return pl.pallas_call(
        functools.partial(_mix_ffn_kernel, tm=tm, shift=shift, pad=pad),
        grid=(B, T // tm),
        in_specs=[row_spec(D), pl.BlockSpec((None, s.shape[1], tm, LANES), lambda b, t: (b, 0, t, 0)),
                  row_spec(attn.shape[2]), row_spec(D), row_spec(D), seq_spec]
        + [_const_spec(a.shape) for a in consts],
        out_specs=[row_spec(D), seq_spec],
        out_shape=[jax.ShapeDtypeStruct((B, T, D), F32), jax.ShapeDtypeStruct((B, hist, F), F32)],
        scratch_shapes=[pltpu.VMEM((pad + tm, F), F32)],
        compiler_params=_params(("arbitrary", "arbitrary")),
        name="mix_ffn_s%d" % shift,
    )(x, s, attn, sga, sgs, prev, *consts)


def _pack_weights(l, g_mix, w_in, b_f, w_glu_a, w_glu_b, w_attn_out, w_out, g_ffn, w_gate, w_up, w_conv, b_conv,
                  w_down, g_final):
    D = w_in.shape[1]
    d_ssm = w_glu_a.shape[1]
    o = [0, D_ATTN, 2 * D_ATTN, 3 * D_ATTN, 3 * D_ATTN + N_HEADS, 3 * D_ATTN + N_HEADS + d_ssm]
    wi = w_in[l]
    wq = wi[:, o[0]:o[1]] * SCALE
    w = {
        "wqkv": jnp.concatenate([wq, wi[:, o[1]:o[3]]], axis=1).astype(BF16),
        "wf": jnp.pad(wi[:, o[3]:o[4]], ((0, 0), (0, LANES - N_HEADS))).astype(BF16),
        "bf": jnp.pad(b_f[l], (0, LANES - N_HEADS)).reshape(1, LANES),
        "wu": wi[:, o[4]:o[5]].astype(BF16),
        "wga": wi[:, o[5]:o[5] + D].astype(BF16),
        "wgs": wi[:, o[5] + D:o[5] + 2 * D].astype(BF16),
        "wa": w_glu_a[l].astype(BF16), "wb": w_glu_b[l].astype(BF16), "wao": w_attn_out[l].astype(BF16),
        "wout": w_out[l].astype(BF16), "gffn": g_ffn[l].reshape(1, D), "wg": w_gate[l].astype(BF16),
        "wup": w_up[l].astype(BF16), "wconv": w_conv[l], "bconv": b_conv[l].reshape(1, -1),
        "wd": w_down[l].astype(BF16), "gfin": g_final.reshape(1, D),
    }
    w["wqt"] = wi[:, o[0]:o[1]].T.astype(BF16)
    w["wvt"] = wi[:, o[2]:o[3]].T.astype(BF16)
    e = jnp.arange(3 * LANES)
    part, head = e // LANES, e % LANES
    colk = head * LANES + HEAD_DIM + part
    cols = jnp.arange(N_HEADS * LANES)[None, :]
    w["selk"] = jnp.where((head < N_HEADS)[:, None] & (cols == colk[:, None]), -1.0, 0.0).astype(BF16)
    return w


def kernel(x_prompt, x_sample, cache_k, cache_v, cache_logf, page_table, state_ssm_re, state_ssm_im, state_conv,
           g_mix, w_in, b_f, lam_re, lam_im, log_dt, b_re, b_im, c_re, c_im, d_skip, w_glu_a, w_glu_b,
           w_attn_out, w_out, g_ffn, w_gate, w_up, w_conv, b_conv, w_down, g_final):
    depth = w_in.shape[0]
    assert depth == 1, "final norm is fused into the layer kernel; one layer supported"
    l = 0
    Bp, T, D = x_prompt.shape
    Bd, Td, _ = x_sample.shape
    n_pool = cache_k.shape[1]
    G, P = lam_re.shape[1], lam_re.shape[2]
    C = SSM_GROUP
    F = w_gate.shape[2]
    R = S5_BLOCK
    assert T % R == 0

    w = _pack_weights(l, g_mix, w_in, b_f, w_glu_a, w_glu_b, w_attn_out, w_out, g_ffn, w_gate, w_up, w_conv,
                      b_conv, w_down, g_final)
    gm = g_mix[l].reshape(1, D)
    sp = {"lam_re": lam_re[l], "lam_im": lam_im[l], "log_dt": log_dt[l], "b_re": b_re[l], "b_im": b_im[l],
          "c_re": c_re[l], "c_im": c_im[l], "d_skip": d_skip[l]}

    k_p, v_p, lf_p, u_p, sga_p, sgs_p, qat, ka, vat = _in_proj(x_prompt, gm, w, prompt=True,
                                                                 attn_blk=_pick_tile(T, 512))
    attn_p = _attn_prompt(qat, ka, vat)
    s_p, hfin = _s5_prompt(u_p, _s5_block_weights(sp))
    zeros_c = jnp.zeros((Bp, CONV_W - 1, F), F32)
    y_p, conv_p = _mix_ffn(x_prompt, s_p, attn_p, sga_p, sgs_p, zeros_c, w, shift=1, tm_pref=256)

    xs_tm = x_sample.transpose(1, 0, 2).reshape(1, Td * Bd, D)
    q_s, k_s, v_s, lf_s, u_s, sga_s, sgs_s = _in_proj(xs_tm, gm, w, prompt=False)
    to_bt = lambda a: a.reshape(Td, Bd, a.shape[-1]).transpose(1, 0, 2)
    q_b, k_b, v_b, lf_b = to_bt(q_s), to_bt(k_s), to_bt(v_s), to_bt(lf_s)
    tpad = lambda a: jnp.pad(a.transpose(0, 2, 1), ((0, 0), (0, 0), (0, PAGE_SIZE - Td)))
    page_t = lambda c: c.transpose(0, 1, 3, 4, 2).reshape(depth, n_pool, D_ATTN, PAGE_SIZE)
    attn_s = _attn_sample(page_table, q_b, tpad(k_b), tpad(v_b), tpad(lf_b), page_t(cache_k), page_t(cache_v),
                          cache_logf.transpose(0, 1, 3, 2), l)
    h0 = jnp.concatenate([state_ssm_re[l].reshape(Bd, G * P), state_ssm_im[l].reshape(Bd, G * P)], axis=1)
    tiles = lambda a: a.reshape(a.shape[0], -1, LANES).transpose(1, 0, 2)
    s_s, ht = _s5_sample(u_s[0].transpose(1, 0, 2).reshape(Td * Bd, G * C), h0, sp, Td)
    attn_s_tm = attn_s.transpose(1, 0, 2).reshape(1, Td * Bd, D_ATTN).astype(BF16)
    prev_s = state_conv[l].transpose(1, 0, 2).reshape(1, (CONV_W - 1) * Bd, F)
    y_s, conv_s = _mix_ffn(xs_tm, tiles(s_s)[None], attn_s_tm, sga_s, sgs_s, prev_s, w,
                           shift=Bd, tm_pref=256)

    y_sample = y_s.reshape(Td, Bd, D).transpose(1, 0, 2)
    conv_sample = conv_s.reshape(CONV_W - 1, Bd, F).transpose(1, 0, 2)
    hd = lambda a, b, t: a.reshape(1, b, t, N_HEADS, HEAD_DIM)
    return (y_p, y_sample,
            hd(k_p, Bp, T), v_p.reshape(Bp, N_HEADS, HEAD_DIM, T).transpose(0, 3, 1, 2)[None],
            lf_p.reshape(1, Bp, T, N_HEADS),
            hd(k_b, Bd, Td), hd(v_b, Bd, Td), lf_b.reshape(1, Bd, Td, N_HEADS),
            hfin[:, :, :P].reshape(1, Bp, G, P), hfin[:, :, P:].reshape(1, Bp, G, P),
            ht[:, :G * P].reshape(1, Bd, G, P), ht[:, G * P:].reshape(1, Bd, G, P),
            conv_p.reshape(1, Bp, CONV_W - 1, F), conv_sample.reshape(1, Bd, CONV_W - 1, F))
```

```python
import functools

import jax
import jax.numpy as jnp
from jax import lax
from jax.experimental import pallas as pl
from jax.experimental.pallas import tpu as pltpu

N_HEADS = 8
HEAD_DIM = 64
D_ATTN = N_HEADS * HEAD_DIM
SSM_GROUP = 16
STATE_P = 64
CONV_W = 3
PAGE_SIZE = 128
RMS_EPS = 1e-6
NEG_INF = -1e30
SCALE = HEAD_DIM ** -0.5
LOG2E = 1.4426950408889634

LANES = 128
VT_ROWS = 80
S5_BLOCK = 16
VMEM_LIMIT = 56 * 1024 * 1024

F32 = jnp.float32
BF16 = jnp.bfloat16
HI = lax.Precision.HIGHEST


def _dot(a, b):
    return jnp.dot(a, b, preferred_element_type=F32)


def _dot_nt(a, b):
    return lax.dot_general(a, b, (((1,), (1,)), ((), ())), preferred_element_type=F32)


def _split3(x):
    hi = x.astype(BF16)
    r1 = x - hi.astype(F32)
    mid = r1.astype(BF16)
    lo = (r1 - mid.astype(F32)).astype(BF16)
    return hi, mid, lo


def _rmsnorm(x, g):
    return x * lax.rsqrt(jnp.mean(x * x, axis=-1, keepdims=True) + RMS_EPS) * g


def _pick_tile(n, pref):
    t = min(n, pref)
    while n % t:
        t //= 2
    return t


def _const_spec(shape):
    nd = len(shape)
    return pl.BlockSpec(shape, lambda *_: (0,) * nd, pipeline_mode=pl.Buffered(1))


def _params(sem):
    return pltpu.CompilerParams(dimension_semantics=sem, vmem_limit_bytes=VMEM_LIMIT)


def _in_proj_kernel(*refs, prompt, tm):
    if prompt:
        (x_ref, g_ref, wqkv_ref, wf_ref, bf_ref, wu_ref, wga_ref, wgs_ref, wqt_ref, wvt_ref, selk_ref,
         k_ref, v_ref, lf_ref, u_ref, sga_ref, sgs_ref, qat_ref, ka_ref, vat_ref, carry_ref) = refs
    else:
        (x_ref, g_ref, wqkv_ref, wf_ref, bf_ref, wu_ref, wga_ref, wgs_ref,
         q_ref, k_ref, v_ref, lf_ref, u_ref, sga_ref, sgs_ref) = refs

    hb = _rmsnorm(x_ref[...], g_ref[...]).astype(BF16)
    logf = jax.nn.log_sigmoid(_dot(hb, wf_ref[...]) + bf_ref[...])
    lf_ref[...] = logf[:, :N_HEADS]
    u = _dot(hb, wu_ref[...])
    for qd in range(u.shape[1] // LANES):
        u_ref[qd] = u[:, qd * LANES:(qd + 1) * LANES]
    sga_ref[...] = jax.nn.sigmoid(_dot(hb, wga_ref[...]))
    sgs_ref[...] = jax.nn.sigmoid(_dot(hb, wgs_ref[...]))
    if not prompt:
        qkv = _dot(hb, wqkv_ref[...])
        q_ref[...] = qkv[:, :D_ATTN]
        k_ref[...] = qkv[:, D_ATTN:2 * D_ATTN]
        v_ref[...] = qkv[:, 2 * D_ATTN:]
        return

    k = _dot(hb, wqkv_ref[:, D_ATTN:2 * D_ATTN])
    k_ref[...] = k

    @pl.when(pl.program_id(1) == 0)
    def _():
        carry_ref[...] = jnp.zeros_like(carry_ref)

    row = lax.broadcasted_iota(jnp.int32, (tm, tm), 0)
    col = lax.broadcasted_iota(jnp.int32, (tm, tm), 1)
    tri = jnp.where(row >= col, 1.0, 0.0).astype(BF16)
    l_hi, l_mid, l_lo = _split3(logf)
    c = _dot(tri, l_hi) + _dot(tri, l_mid) + _dot(tri, l_lo) + carry_ref[...]
    carry_ref[...] = c[tm - 1:tm, :]

    e = jnp.concatenate(_split3(c * LOG2E), axis=1)
    ek = _dot(e, selk_ref[...])
    lane = lax.broadcasted_iota(jnp.int32, (tm, LANES), 1)
    low = lane < HEAD_DIM
    for h in range(N_HEADS):
        t = h // 2
        kt = k[:, t * LANES:(t + 1) * LANES]
        if h % 2:
            kt = pltpu.roll(kt, HEAD_DIM, 1)
        ka_ref[h] = jnp.where(low, kt, ek[:, h * LANES:(h + 1) * LANES]).astype(BF16)
    qt = _dot_nt(wqt_ref[...], hb) * (SCALE * LOG2E)
    vt = _dot_nt(wvt_ref[...], hb)
    v_ref[...] = vt
    sub = lax.broadcasted_iota(jnp.int32, (HEAD_DIM, tm), 0)
    q_ones = jnp.where(sub < 3, 1.0, 0.0)
    sub_v = lax.broadcasted_iota(jnp.int32, (VT_ROWS - HEAD_DIM, tm), 0)
    v_ones = jnp.where(sub_v == 0, 1.0, 0.0)
    for h in range(N_HEADS):
        rows = slice(h * HEAD_DIM, (h + 1) * HEAD_DIM)
        qat_ref[h] = jnp.concatenate([qt[rows], q_ones], axis=0).astype(BF16)
        vat_ref[h] = jnp.concatenate([vt[rows], v_ones], axis=0).astype(BF16)


def _in_proj(x, g_mix, w, *, prompt, attn_blk=None):
    B, T, D = x.shape
    tm = _pick_tile(T, 256)
    nt = T // tm
    d_gate = w["wga"].shape[1]
    d_ssm = w["wu"].shape[1]
    row_spec = lambda n: pl.BlockSpec((None, tm, n), lambda b, t: (b, t, 0))
    ins = [x, g_mix, w["wqkv"], w["wf"], w["bf"], w["wu"], w["wga"], w["wgs"]]
    in_specs = [row_spec(D)] + [_const_spec(a.shape) for a in ins[1:]]
    outs = [jax.ShapeDtypeStruct((B, T, D_ATTN), F32),
            jax.ShapeDtypeStruct((B, T, D_ATTN), F32),
            jax.ShapeDtypeStruct((B, T, N_HEADS), F32),
            jax.ShapeDtypeStruct((B, d_ssm // LANES, T, LANES), F32),
            jax.ShapeDtypeStruct((B, T, d_gate), F32),
            jax.ShapeDtypeStruct((B, T, d_gate), F32)]
    out_specs = [row_spec(D_ATTN), row_spec(D_ATTN), row_spec(N_HEADS),
                 pl.BlockSpec((None, d_ssm // LANES, tm, LANES), lambda b, t: (b, 0, t, 0)),
                 row_spec(d_gate), row_spec(d_gate)]
    scratch = []
    if prompt:
        outs[1] = jax.ShapeDtypeStruct((B, D_ATTN, T), F32)
        out_specs[1] = pl.BlockSpec((None, D_ATTN, tm), lambda b, t: (b, 0, t))
        extra = [w["wqt"], w["wvt"], w["selk"]]
        ins += extra
        in_specs += [_const_spec(a.shape) for a in extra]
        per = attn_blk // tm
        outs += [jax.ShapeDtypeStruct((B, N_HEADS, LANES, T), BF16),
                 jax.ShapeDtypeStruct((B, N_HEADS, T, LANES), BF16),
                 jax.ShapeDtypeStruct((B, N_HEADS, T // attn_blk, VT_ROWS, attn_blk), BF16)]
        out_specs += [pl.BlockSpec((None, N_HEADS, LANES, tm), lambda b, t: (b, 0, 0, t)),
                      pl.BlockSpec((None, N_HEADS, tm, LANES), lambda b, t: (b, 0, t, 0)),
                      pl.BlockSpec((None, N_HEADS, None, VT_ROWS, tm), lambda b, t: (b, 0, t // per, 0, t % per))]
        scratch = [pltpu.VMEM((1, LANES), F32)]
    else:
        outs = [jax.ShapeDtypeStruct((B, T, D_ATTN), F32)] + outs
        out_specs = [row_spec(D_ATTN)] + out_specs
    return pl.pallas_call(
        functools.partial(_in_proj_kernel, prompt=prompt, tm=tm),
        grid=(B, nt), in_specs=in_specs, out_specs=out_specs, out_shape=outs, scratch_shapes=scratch,
        compiler_params=_params(("arbitrary", "arbitrary")),
        name="in_proj_prompt" if prompt else "in_proj_sample",
    )(*ins)


def _attn_prompt_kernel(qt_ref, k_ref, vt_ref, o_ref, acc0_ref, acc1_ref, *, bq, bk):
    qi = pl.program_id(2)
    ratio = bq // bk
    accs = (acc0_ref, acc1_ref)
    for a in accs:
        a[...] = jnp.zeros_like(a)

    def update(j, ms, diag):
        start = pl.multiple_of(j * bk, bk)
        q0 = 0 if diag is None else diag * bk
        new_ms = []
        for hh in range(2):
            st = _dot(k_ref[hh, pl.ds(start, bk), :], qt_ref[hh, :, q0:])
            if diag is not None:
                key = lax.broadcasted_iota(jnp.int32, st.shape, 0)
                qry = lax.broadcasted_iota(jnp.int32, st.shape, 1)
                st = jnp.where(key <= qry, st, NEG_INF)
            m_old = ms[hh][:, q0:]
            m_new = jnp.maximum(m_old, jnp.max(st, axis=0, keepdims=True))
            p = jnp.exp2(st - m_new).astype(BF16)
            accs[hh][:, q0:] = jnp.exp2(m_old - m_new) * accs[hh][:, q0:] + _dot(vt_ref[hh, j], p)
            new_ms.append(m_new if q0 == 0 else jnp.concatenate([ms[hh][:, :q0], m_new], axis=1))
        return tuple(new_ms)

    m0 = jnp.full((1, bq), NEG_INF, F32)
    ms = lax.fori_loop(0, qi * ratio, lambda j, ms: update(j, ms, None), (m0, m0))
    for d in range(ratio):
        ms = update(qi * ratio + d, ms, d)
    outs = []
    for a in accs:
        acc = a[...]
        outs.append(acc[:HEAD_DIM] / acc[HEAD_DIM:HEAD_DIM + 1])
    o_ref[...] = jnp.concatenate(outs, axis=0).T.astype(o_ref.dtype)


def _attn_prompt(qat, ka, vat):
    B, H, T, _ = ka.shape
    nk, bk = vat.shape[2], vat.shape[4]
    bq = _pick_tile(T, 4 * bk)
    return pl.pallas_call(
        functools.partial(_attn_prompt_kernel, bq=bq, bk=bk),
        grid=(B, H // 2, T // bq),
        in_specs=[pl.BlockSpec((None, 2, LANES, bq), lambda b, hp, qi: (b, hp, 0, qi)),
                  pl.BlockSpec((None, 2, T, LANES), lambda b, hp, qi: (b, hp, 0, 0)),
                  pl.BlockSpec((None, 2, nk, VT_ROWS, bk), lambda b, hp, qi: (b, hp, 0, 0, 0))],
        out_specs=pl.BlockSpec((None, bq, LANES), lambda b, hp, qi: (b, qi, hp)),
        out_shape=jax.ShapeDtypeStruct((B, T, D_ATTN), BF16),
        scratch_shapes=[pltpu.VMEM((VT_ROWS, bq), F32), pltpu.VMEM((VT_ROWS, bq), F32)],
        compiler_params=_params(("arbitrary", "arbitrary", "arbitrary")),
        name="attn_prompt",
    )(qat, ka, vat)


def _attn_sample_kernel(pt_ref, q_ref, kn_ref, vn_ref, lfn_ref, *rest, pg, tq):
    k_refs = rest[:pg]
    v_refs = rest[pg:2 * pg]
    lf_refs = rest[2 * pg:3 * pg]
    o_ref, qbd_ref, acc_ref, m_ref, l_ref, coff_ref = rest[3 * pg:]
    del pt_ref
    j = pl.program_id(1)
    hq = N_HEADS * tq

    @pl.when(j == 0)
    def _():
        q = q_ref[...] * LOG2E
        lane_head = lax.broadcasted_iota(jnp.int32, (tq, D_ATTN), 1) // HEAD_DIM
        qbd_ref[...] = jnp.concatenate(
            [jnp.where(lane_head == h, q, 0.0) for h in range(N_HEADS)], axis=0).astype(BF16)
        acc_ref[...] = jnp.zeros_like(acc_ref)
        l_ref[...] = jnp.zeros_like(l_ref)
        m_ref[...] = jnp.full_like(m_ref, NEG_INF)
        coff_ref[...] = jnp.zeros_like(coff_ref)

    r = lax.broadcasted_iota(jnp.int32, (PAGE_SIZE, PAGE_SIZE), 0)
    cc = lax.broadcasted_iota(jnp.int32, (PAGE_SIZE, PAGE_SIZE), 1)
    triu = jnp.where(r <= cc, 1.0, 0.0).astype(BF16)
    qbd = qbd_ref[...]

    def cum_logf(lf_list, coff):
        n = len(lf_list)
        lf = jnp.concatenate(lf_list, axis=0) if n > 1 else lf_list[0]
        a, b, c3 = _split3(lf)
        w = _dot(a, triu) + _dot(b, triu) + _dot(c3, triu)
        tot = jnp.broadcast_to(w[:, PAGE_SIZE - 1:PAGE_SIZE], w.shape)
        pages = []
        for i in range(n):
            pages.append(w[i * N_HEADS:(i + 1) * N_HEADS] + coff)
            coff = coff + tot[i * N_HEADS:(i + 1) * N_HEADS]
        return pages, coff

    def page_scores(kt_page, ck):
        s = _dot(qbd, kt_page.astype(BF16))
        bias = jnp.concatenate(
            [jnp.broadcast_to(ck[h:h + 1, :], (tq, PAGE_SIZE)) for h in range(N_HEADS)], axis=0)
        return s - bias * LOG2E

    def online_update(s_list, vt_list):
        s = jnp.concatenate(s_list, axis=1) if len(s_list) > 1 else s_list[0]
        m_prev = m_ref[...]
        m_new = jnp.maximum(m_prev, jnp.max(s, axis=1, keepdims=True))
        p = jnp.exp2(s - m_new)
        alpha = jnp.exp2(m_prev - m_new)
        pv = None
        for i, vt_page in enumerate(vt_list):
            d = _dot_nt(p[:, i * PAGE_SIZE:(i + 1) * PAGE_SIZE].astype(BF16), vt_page.astype(BF16))
            pv = d if pv is None else pv + d
        acc_ref[...] = alpha * acc_ref[...] + pv
        l_ref[...] = alpha * l_ref[...] + jnp.sum(p, axis=1, keepdims=True)
        m_ref[...] = m_new

    cks, coff = cum_logf([lf_refs[i][...] for i in range(pg)], coff_ref[...])
    coff_ref[...] = coff
    online_update([page_scores(k_refs[i][...], cks[i]) for i in range(pg)], [v_refs[i][...] for i in range(pg)])

    @pl.when(j == pl.num_programs(1) - 1)
    def _():
        cnew, _unused = cum_logf([lfn_ref[...]], coff_ref[...])
        s = page_scores(kn_ref[...], cnew[0])
        key = lax.broadcasted_iota(jnp.int32, (hq, PAGE_SIZE), 1)
        qtok = lax.broadcasted_iota(jnp.int32, (hq, PAGE_SIZE), 0) % tq
        online_update([jnp.where(key <= qtok, s, NEG_INF)], [vn_ref[...]])
        o = acc_ref[...] / l_ref[...]
        lane = lax.broadcasted_iota(jnp.int32, (tq, LANES), 1)
        tiles = []
        for t in range(N_HEADS // 2):
            a = o[(2 * t) * tq:(2 * t + 1) * tq, t * LANES:(t + 1) * LANES]
            b = o[(2 * t + 1) * tq:(2 * t + 2) * tq, t * LANES:(t + 1) * LANES]
            tiles.append(jnp.where(lane < HEAD_DIM, a, b))
        o_ref[...] = jnp.concatenate(tiles, axis=1)


def _attn_sample(page_table, q, kt_new, vt_new, lft_new, cache_kt, cache_vt, cache_lft, layer):
    Bd, tq, _ = q.shape
    n_pages = page_table.shape[1]
    pg = _pick_tile(n_pages, 32)
    ng = n_pages // pg
    pt_flat = page_table.reshape(-1)

    def page_spec(rows, i):
        return pl.BlockSpec((None, None, rows, PAGE_SIZE),
                            lambda b, j, pt: (layer, pt[b * n_pages + j * pg + i], 0, 0))

    seq_spec = lambda shape: pl.BlockSpec((None,) + shape, lambda b, j, pt: (b, 0, 0))
    in_specs = ([seq_spec((tq, D_ATTN)), seq_spec((D_ATTN, PAGE_SIZE)), seq_spec((D_ATTN, PAGE_SIZE)),
                 seq_spec((N_HEADS, PAGE_SIZE))]
                + [page_spec(D_ATTN, i) for i in range(pg)] + [page_spec(D_ATTN, i) for i in range(pg)]
                + [page_spec(N_HEADS, i) for i in range(pg)])
    hq = N_HEADS * tq
    grid_spec = pltpu.PrefetchScalarGridSpec(
        num_scalar_prefetch=1, grid=(Bd, ng), in_specs=in_specs,
        out_specs=pl.BlockSpec((None, tq, D_ATTN), lambda b, j, pt: (b, 0, 0)),
        scratch_shapes=[pltpu.VMEM((hq, D_ATTN), BF16), pltpu.VMEM((hq, D_ATTN), F32),
                        pltpu.VMEM((hq, 1), F32), pltpu.VMEM((hq, 1), F32), pltpu.VMEM((N_HEADS, PAGE_SIZE), F32)])
    return pl.pallas_call(
        functools.partial(_attn_sample_kernel, pg=pg, tq=tq),
        grid_spec=grid_spec,
        out_shape=jax.ShapeDtypeStruct((Bd, tq, D_ATTN), F32),
        compiler_params=_params(("arbitrary", "arbitrary")),
        name="attn_sample",
    )(pt_flat, q, kt_new, vt_new, lft_new, *([cache_kt] * pg), *([cache_vt] * pg), *([cache_lft] * pg))


def _s5_discretize(p):
    lr, li = p["lam_re"], p["lam_im"]
    dt = jnp.exp(p["log_dt"])[:, None]
    mag = jnp.exp(lr * dt)
    ang = li * dt
    ab_re = mag * jnp.cos(ang)
    ab_im = mag * jnp.sin(ang)
    den = lr * lr + li * li
    n_re = ab_re - 1.0
    n_im = ab_im
    f_re = (n_re * lr + n_im * li) / den
    f_im = (n_im * lr - n_re * li) / den
    bb_re = f_re[..., None] * p["b_re"] - f_im[..., None] * p["b_im"]
    bb_im = f_re[..., None] * p["b_im"] + f_im[..., None] * p["b_re"]
    return ab_re, ab_im, bb_re, bb_im


def _s5_block_weights(p):
    R = S5_BLOCK
    ab_re, ab_im, bb_re, bb_im = _s5_discretize(p)
    G, P, C = bb_re.shape
    pw_re, pw_im = [jnp.ones_like(ab_re)], [jnp.zeros_like(ab_im)]
    for _ in range(R):
        pr, pi = pw_re[-1], pw_im[-1]
        pw_re.append(pr * ab_re - pi * ab_im)
        pw_im.append(pr * ab_im + pi * ab_re)
    pw_re = jnp.stack(pw_re)
    pw_im = jnp.stack(pw_im)
    c_re, c_im = p["c_re"], p["c_im"]
    cp_re = c_re[None] * pw_re[:, :, None, :] - c_im[None] * pw_im[:, :, None, :]
    cp_im = c_re[None] * pw_im[:, :, None, :] + c_im[None] * pw_re[:, :, None, :]
    m = (jnp.einsum('jgcp,gpd->jgdc', cp_re[:R], bb_re, precision=HI)
         - jnp.einsum('jgcp,gpd->jgdc', cp_im[:R], bb_im, precision=HI))
    lag = jnp.arange(R)[None, :] - jnp.arange(R)[:, None]
    onehot = (lag[None] == jnp.arange(R)[:, None, None]).astype(F32)
    kt = jnp.einsum('jab,jgdc->gadbc', onehot, m, precision=HI).reshape(G, R * C, R * C)
    pb_re = pw_re[R - 1::-1][:R]
    pb_im = pw_im[R - 1::-1][:R]
    x_re = pb_re[..., None] * bb_re[None] - pb_im[..., None] * bb_im[None]
    x_im = pb_re[..., None] * bb_im[None] + pb_im[..., None] * bb_re[None]
    x_re = x_re.transpose(1, 0, 3, 2).reshape(G, R * C, P)
    x_im = x_im.transpose(1, 0, 3, 2).reshape(G, R * C, P)
    wx = jnp.concatenate([x_re, x_im, x_im, x_re], axis=-1)
    wc_re = cp_re[1:].transpose(1, 3, 0, 2).reshape(G, P, R * C)
    wc_im = cp_im[1:].transpose(1, 3, 0, 2).reshape(G, P, R * C)
    wc = jnp.concatenate([wc_re, -wc_im], axis=1)
    a_blk = jnp.concatenate([pw_re[R], pw_re[R]], axis=-1)
    b_blk = jnp.concatenate([-pw_im[R], pw_im[R]], axis=-1)
    d = jnp.tile(p["d_skip"].reshape(G, 1, C), (1, 1, R))
    return kt.astype(BF16), wx.astype(BF16), wc.astype(BF16), a_blk, b_blk, d


def _s5_prompt_kernel(u_ref, kt_ref, wx_ref, wc_ref, a_ref, b_ref, d_ref, y_ref, hfin_ref,
                      ug_ref, xa_ref, xb_ref, hp_ref, h_ref, hsw_ref, *, tb, ng):
    R, C = S5_BLOCK, SSM_GROUP
    gpt = LANES // C
    sp = 2 * STATE_P

    @pl.when(pl.program_id(1) == 0)
    def _():
        h_ref[...] = jnp.zeros_like(h_ref)
        hsw_ref[...] = jnp.zeros_like(hsw_ref)

    lane_chunk = lax.broadcasted_iota(jnp.int32, (tb, LANES), 1) // C

    def chunk_transpose(vs):
        vs = list(vs)
        step = gpt // 2
        while step >= 1:
            low = (lane_chunk // step) % 2 == 0
            for i in range(gpt):
                if (i // step) % 2 == 0:
                    a, b = vs[i], vs[i + step]
                    vs[i] = jnp.where(low, a, pltpu.roll(b, step * C, 1))
                    vs[i + step] = jnp.where(low, pltpu.roll(a, LANES - step * C, 1), b)
            step //= 2
        return vs

    for qd in range(ng // gpt):
        for half in range(R // gpt):
            toks = [u_ref[qd, pl.ds(half * gpt + r8, tb, stride=R), :] for r8 in range(gpt)]
            for gl, v in enumerate(chunk_transpose(toks)):
                ug_ref[qd * gpt + gl, :, half * LANES:(half + 1) * LANES] = v

    def xbody(g, carry):
        x = _dot(ug_ref[g].astype(BF16), wx_ref[g])
        rows = pl.ds(pl.multiple_of(g * tb, tb), tb)
        xa_ref[rows, :] = x[:, :sp]
        xb_ref[rows, :] = x[:, sp:]
        return carry

    lax.fori_loop(0, ng, xbody, 0, unroll=4)

    a = a_ref[...]
    b = b_ref[...]

    def sbody(k, carry):
        h, hsw = carry
        rows = pl.ds(k, ng, stride=tb)
        hp_ref[rows, :] = h
        return a * h + b * hsw + xa_ref[rows, :], a * hsw - b * h + xb_ref[rows, :]

    h, hsw = lax.fori_loop(0, tb, sbody, (h_ref[...], hsw_ref[...]), unroll=8)
    h_ref[...] = h
    hsw_ref[...] = hsw
    hfin_ref[...] = h

    def ybody(g, carry):
        u = ug_ref[g]
        hp = hp_ref[pl.ds(pl.multiple_of(g * tb, tb), tb), :].astype(BF16)
        ug_ref[g] = _dot(u.astype(BF16), kt_ref[g]) + _dot(hp, wc_ref[g]) + u * d_ref[g]
        return carry

    lax.fori_loop(0, ng, ybody, 0, unroll=4)

    for qd in range(ng // gpt):
        for half in range(R // gpt):
            grps = [ug_ref[qd * gpt + gl, :, half * LANES:(half + 1) * LANES] for gl in range(gpt)]
            for r8, v in enumerate(chunk_transpose(grps)):
                y_ref[qd, pl.ds(half * gpt + r8, tb, stride=R), :] = v


def _s5_prompt(u, weights):
    kt, wx, wc, a_blk, b_blk, d = weights
    B, nq, T, _ = u.shape
    G = kt.shape[0]
    R = S5_BLOCK
    tb = _pick_tile(T // R, 128)
    sp = 2 * STATE_P
    blk_spec = pl.BlockSpec((None, nq, tb * R, LANES), lambda b, t: (b, 0, t, 0))
    return pl.pallas_call(
        functools.partial(_s5_prompt_kernel, tb=tb, ng=G),
        grid=(B, T // (tb * R)),
        in_specs=[blk_spec] + [_const_spec(x.shape) for x in (kt, wx, wc, a_blk, b_blk, d)],
        out_specs=[blk_spec, pl.BlockSpec((None, G, sp), lambda b, t: (b, 0, 0))],
        out_shape=[jax.ShapeDtypeStruct(u.shape, F32), jax.ShapeDtypeStruct((B, G, sp), F32)],
        scratch_shapes=[pltpu.VMEM((G, tb, R * SSM_GROUP), F32),
                        pltpu.VMEM((G * tb, sp), F32), pltpu.VMEM((G * tb, sp), F32), pltpu.VMEM((G * tb, sp), F32),
                        pltpu.VMEM((G, sp), F32), pltpu.VMEM((G, sp), F32)],
        compiler_params=_params(("arbitrary", "arbitrary")),
        name="s5_prompt",
    )(u, kt, wx, wc, a_blk, b_blk, d)


def _s5_sample_kernel(u_ref, h0_ref, bbd_ref, cmat_ref, are_ref, aim_ref, d_ref, y_ref, ht_ref, hs_ref, *, nt, nb):
    ns = are_ref.shape[1]
    u = u_ref[...]
    ub = u.astype(BF16)
    cw = 256
    for c0 in range(0, ns, cw):
        x_re = _dot(ub, bbd_ref[:, c0:c0 + cw])
        x_im = _dot(ub, bbd_ref[:, ns + c0:ns + c0 + cw])
        are = are_ref[:, c0:c0 + cw]
        aim = aim_ref[:, c0:c0 + cw]
        hr = h0_ref[:, c0:c0 + cw]
        hi = h0_ref[:, ns + c0:ns + c0 + cw]
        for t in range(nt):
            rows = slice(t * nb, (t + 1) * nb)
            hr, hi = are * hr - aim * hi + x_re[rows], are * hi + aim * hr + x_im[rows]
            hs_ref[rows, c0:c0 + cw] = hr.astype(BF16)
            hs_ref[rows, ns + c0:ns + c0 + cw] = hi.astype(BF16)
        ht_ref[:, c0:c0 + cw] = hr
        ht_ref[:, ns + c0:ns + c0 + cw] = hi
    y_ref[...] = _dot(hs_ref[...], cmat_ref[...]) + u * d_ref[...]


def _s5_sample(u_tm, h0, p, nt):
    ab_re, ab_im, bb_re, bb_im = _s5_discretize(p)
    G, P, C = bb_re.shape
    eye = jnp.eye(G, dtype=F32)
    bd = lambda w: jnp.einsum('gpc,gk->gckp', w, eye).reshape(G * C, G * P)
    bbd = jnp.concatenate([bd(bb_re), bd(bb_im)], axis=1).astype(BF16)
    cd = lambda w: jnp.einsum('gcp,gk->gpkc', w, eye).reshape(G * P, G * C)
    cmat = jnp.concatenate([cd(p["c_re"]), -cd(p["c_im"])], axis=0).astype(BF16)
    nb = h0.shape[0]
    ins = [u_tm, h0, bbd, cmat, ab_re.reshape(1, G * P), ab_im.reshape(1, G * P), p["d_skip"].reshape(1, G * C)]
    return pl.pallas_call(
        functools.partial(_s5_sample_kernel, nt=nt, nb=nb),
        grid=(1,),
        in_specs=[_const_spec(x.shape) for x in ins],
        out_specs=[pl.BlockSpec(u_tm.shape, lambda i: (0, 0)), pl.BlockSpec(h0.shape, lambda i: (0, 0))],
        out_shape=[jax.ShapeDtypeStruct(u_tm.shape, F32), jax.ShapeDtypeStruct(h0.shape, F32)],
        scratch_shapes=[pltpu.VMEM((nt * nb, 2 * G * P), BF16)],
        compiler_params=_params(("arbitrary",)),
        name="s5_sample",
    )(*ins)


def _mix_ffn_kernel(x_ref, s_ref, attn_ref, sga_ref, sgs_ref, prev_ref,
                    wa_ref, wb_ref, wao_ref, wout_ref, gffn_ref, wg_ref, wup_ref, wconv_ref, bconv_ref, wd_ref,
                    gfin_ref, y_ref, conv_ref, gate_ref, *, tm, shift, pad):
    hist = (CONV_W - 1) * shift

    @pl.when(pl.program_id(1) == 0)
    def _():
        gate_ref[pad - hist:pad, :] = prev_ref[...]

    s = jnp.concatenate([s_ref[qd] for qd in range(s_ref.shape[0])], axis=1)
    zb = jax.nn.gelu(s).astype(BF16)
    ssm_out = _dot(zb, wa_ref[...]) * jax.nn.sigmoid(_dot(zb, wb_ref[...]))
    attn_out = _dot(attn_ref[...], wao_ref[...])
    merged = sgs_ref[...] * ssm_out + sga_ref[...] * attn_out
    x1 = x_ref[...] + _dot(merged.astype(BF16), wout_ref[...])
    h2 = _rmsnorm(x1, gffn_ref[...]).astype(BF16)
    gate_ref[pad:pad + tm, :] = _dot(h2, wg_ref[...])
    gc = gate_ref[pad - hist:pad - hist + tm, :] * wconv_ref[0:1, :]
    for i in range(1, CONV_W):
        gc = gc + gate_ref[pad - hist + i * shift:pad - hist + i * shift + tm, :] * wconv_ref[i:i + 1, :]
    gc = bconv_ref[...] + gc
    act = (jax.nn.gelu(gc) * _dot(h2, wup_ref[...])).astype(BF16)
    x2 = x1 + _dot(act, wd_ref[...])
    y_ref[...] = _rmsnorm(x2, gfin_ref[...])
    tail = gate_ref[pad + tm - hist:pad + tm, :]
    conv_ref[...] = tail
    gate_ref[pad - hist:pad, :] = tail


def _mix_ffn(x, s, attn, sga, sgs, prev, w, *, shift, tm_pref):
    B, T, D = x.shape
    tm = _pick_tile(T, tm_pref)
    hist = (CONV_W - 1) * shift
    pad = -(-hist // 8) * 8
    F = w["wg"].shape[1]
    row_spec = lambda n: pl.BlockSpec((None, tm, n), lambda b, t: (b, t, 0))
    seq_spec = pl.BlockSpec((None, hist, F), lambda b, t: (b, 0, 0))
    consts = [w["wa"], w["wb"], w["wao"], w["wout"], w["gffn"], w["wg"], w["wup"], w["wconv"], w["bconv"], w["wd"],
              w["gfin"]]
    return pl.pallas_call(
        functools.partial(_mix_ffn_kernel, tm=tm, shift=shift, pad=pad),
        grid=(B, T // tm),
        in_specs=[row_spec(D), pl.BlockSpec((None, s.shape[1], tm, LANES), lambda b, t: (b, 0, t, 0)),
                  row_spec(attn.shape[2]), row_spec(D), row_spec(D), seq_spec]
        + [_const_spec(a.shape) for a in consts],
        out_specs=[row_spec(D), seq_spec],
        out_shape=[jax.ShapeDtypeStruct((B, T, D), F32), jax.ShapeDtypeStruct((B, hist, F), F32)],
        scratch_shapes=[pltpu.VMEM((pad + tm, F), F32)],
        compiler_params=_params(("arbitrary", "arbitrary")),
        name="mix_ffn_s%d" % shift,
    )(x, s, attn, sga, sgs, prev, *consts)


def _pack_weights(l, g_mix, w_in, b_f, w_glu_a, w_glu_b, w_attn_out, w_out, g_ffn, w_gate, w_up, w_conv, b_conv,
                  w_down, g_final):
    D = w_in.shape[1]
    d_ssm = w_glu_a.shape[1]
    o = [0, D_ATTN, 2 * D_ATTN, 3 * D_ATTN, 3 * D_ATTN + N_HEADS, 3 * D_ATTN + N_HEADS + d_ssm]
    wi = w_in[l]
    wq = wi[:, o[0]:o[1]] * SCALE
    w = {
        "wqkv": jnp.concatenate([wq, wi[:, o[1]:o[3]]], axis=1).astype(BF16),
        "wf": jnp.pad(wi[:, o[3]:o[4]], ((0, 0), (0, LANES - N_HEADS))).astype(BF16),
        "bf": jnp.pad(b_f[l], (0, LANES - N_HEADS)).reshape(1, LANES),
        "wu": wi[:, o[4]:o[5]].astype(BF16),
        "wga": wi[:, o[5]:o[5] + D].astype(BF16),
        "wgs": wi[:, o[5] + D:o[5] + 2 * D].astype(BF16),
        "wa": w_glu_a[l].astype(BF16), "wb": w_glu_b[l].astype(BF16), "wao": w_attn_out[l].astype(BF16),
        "wout": w_out[l].astype(BF16), "gffn": g_ffn[l].reshape(1, D), "wg": w_gate[l].astype(BF16),
        "wup": w_up[l].astype(BF16), "wconv": w_conv[l], "bconv": b_conv[l].reshape(1, -1),
        "wd": w_down[l].astype(BF16), "gfin": g_final.reshape(1, D),
    }
    w["wqt"] = wi[:, o[0]:o[1]].T.astype(BF16)
    w["wvt"] = wi[:, o[2]:o[3]].T.astype(BF16)
    e = jnp.arange(3 * LANES)
    part, head = e // LANES, e % LANES
    colk = head * LANES + HEAD_DIM + part
    cols = jnp.arange(N_HEADS * LANES)[None, :]
    w["selk"] = jnp.where((head < N_HEADS)[:, None] & (cols == colk[:, None]), -1.0, 0.0).astype(BF16)
    return w


def kernel(x_prompt, x_sample, cache_k, cache_v, cache_logf, page_table, state_ssm_re, state_ssm_im, state_conv,
           g_mix, w_in, b_f, lam_re, lam_im, log_dt, b_re, b_im, c_re, c_im, d_skip, w_glu_a, w_glu_b,
           w_attn_out, w_out, g_ffn, w_gate, w_up, w_conv, b_conv, w_down, g_final):
    depth = w_in.shape[0]
    assert depth == 1, "final norm is fused into the layer kernel; one layer supported"
    l = 0
    Bp, T, D = x_prompt.shape
    Bd, Td, _ = x_sample.shape
    n_pool = cache_k.shape[1]
    G, P = lam_re.shape[1], lam_re.shape[2]
    C = SSM_GROUP
    F = w_gate.shape[2]
    R = S5_BLOCK
    assert T % R == 0

    w = _pack_weights(l, g_mix, w_in, b_f, w_glu_a, w_glu_b, w_attn_out, w_out, g_ffn, w_gate, w_up, w_conv,
                      b_conv, w_down, g_final)
    gm = g_mix[l].reshape(1, D)
    sp = {"lam_re": lam_re[l], "lam_im": lam_im[l], "log_dt": log_dt[l], "b_re": b_re[l], "b_im": b_im[l],
          "c_re": c_re[l], "c_im": c_im[l], "d_skip": d_skip[l]}

    k_p, v_p, lf_p, u_p, sga_p, sgs_p, qat, ka, vat = _in_proj(x_prompt, gm, w, prompt=True,
                                                                 attn_blk=_pick_tile(T, 512))
    attn_p = _attn_prompt(qat, ka, vat)
    s_p, hfin = _s5_prompt(u_p, _s5_block_weights(sp))
    zeros_c = jnp.zeros((Bp, CONV_W - 1, F), F32)
    y_p, conv_p = _mix_ffn(x_prompt, s_p, attn_p, sga_p, sgs_p, zeros_c, w, shift=1, tm_pref=256)

    xs_tm = x_sample.transpose(1, 0, 2).reshape(1, Td * Bd, D)
    q_s, k_s, v_s, lf_s, u_s, sga_s, sgs_s = _in_proj(xs_tm, gm, w, prompt=False)
    to_bt = lambda a: a.reshape(Td, Bd, a.shape[-1]).transpose(1, 0, 2)
    q_b, k_b, v_b, lf_b = to_bt(q_s), to_bt(k_s), to_bt(v_s), to_bt(lf_s)
    tpad = lambda a: jnp.pad(a.transpose(0, 2, 1), ((0, 0), (0, 0), (0, PAGE_SIZE - Td)))
    page_t = lambda c: c.transpose(0, 1, 3, 4, 2).reshape(depth, n_pool, D_ATTN, PAGE_SIZE)
    attn_s = _attn_sample(page_table, q_b, tpad(k_b.astype(BF16)), tpad(v_b.astype(BF16)), tpad(lf_b),
                          page_t(cache_k), page_t(cache_v),
                          cache_logf.transpose(0, 1, 3, 2), l)
    h0 = jnp.concatenate([state_ssm_re[l].reshape(Bd, G * P), state_ssm_im[l].reshape(Bd, G * P)], axis=1)
    tiles = lambda a: a.reshape(a.shape[0], -1, LANES).transpose(1, 0, 2)
    s_s, ht = _s5_sample(u_s[0].transpose(1, 0, 2).reshape(Td * Bd, G * C), h0, sp, Td)
    attn_s_tm = attn_s.transpose(1, 0, 2).reshape(1, Td * Bd, D_ATTN).astype(BF16)
    prev_s = state_conv[l].transpose(1, 0, 2).reshape(1, (CONV_W - 1) * Bd, F)
    y_s, conv_s = _mix_ffn(xs_tm, tiles(s_s)[None], attn_s_tm, sga_s, sgs_s, prev_s, w,
                           shift=Bd, tm_pref=256)

    y_sample = y_s.reshape(Td, Bd, D).transpose(1, 0, 2)
    conv_sample = conv_s.reshape(CONV_W - 1, Bd, F).transpose(1, 0, 2)
    hd = lambda a, b, t: a.reshape(1, b, t, N_HEADS, HEAD_DIM)
    return (y_p, y_sample,
            hd(k_p, Bp, T), v_p.reshape(Bp, N_HEADS, HEAD_DIM, T).transpose(0, 3, 1, 2)[None],
            lf_p.reshape(1, Bp, T, N_HEADS),
            hd(k_b, Bd, Td), hd(v_b, Bd, Td), lf_b.reshape(1, Bd, Td, N_HEADS),
            hfin[:, :, :P].reshape(1, Bp, G, P), hfin[:, :, P:].reshape(1, Bp, G, P),
            ht[:, :G * P].reshape(1, Bd, G, P), ht[:, G * P:].reshape(1, Bd, G, P),
            conv_p.reshape(1, Bp, CONV_W - 1, F), conv_sample.reshape(1, Bd, CONV_W - 1, F))
```
